```python
import jax, jax.numpy as jnp
from jax import lax
import numpy as np

D_MODEL = 1024
BATCH = 32
SEQ = 2048
DEPTH = 1

D_CONV = D_MODEL
CONV_WIDTH = 3
HEAD_DIM = 64
N_HEADS = D_MODEL // HEAD_DIM
N_KV_HEADS = 2
GROUP = N_HEADS // N_KV_HEADS
D_ATTN = N_HEADS * HEAD_DIM
D_KV = N_KV_HEADS * HEAD_DIM
WINDOW = 128
BLOCK = WINDOW
ROPE_THETA = 10000.0
RMS_EPS = 1e-6
SPLITS = (D_CONV, D_CONV, D_CONV, D_CONV, D_ATTN, D_KV, D_KV, D_ATTN, D_MODEL, D_MODEL)
D_IN = int(sum(SPLITS))
SPLIT_IDX = [int(s) for s in np.cumsum(SPLITS)[:-1]]

kernel_name = "hybrid_shortconv_swa_sink_gated_merge"


def rmsnorm(x, g):
    xf = x.astype(jnp.float32)
    y = xf * lax.rsqrt(jnp.mean(xf * xf, axis=-1, keepdims=True) + RMS_EPS)
    return (y * g.astype(jnp.float32)).astype(x.dtype)


def short_gated_conv(xc, bg, cg, w_conv):
    u = cg * xc
    t = u.shape[1]
    u_pad = jnp.pad(u, ((0, 0), (CONV_WIDTH - 1, 0), (0, 0)))
    y = w_conv[0] * u_pad[:, 0:t]
    for tap in range(1, CONV_WIDTH):
        y = y + w_conv[tap] * u_pad[:, tap:tap + t]
    return bg * y


def rope(z, positions):
    inv_freq = ROPE_THETA ** (-jnp.arange(0, HEAD_DIM, 2, dtype=jnp.float32) / HEAD_DIM)
    ang = positions.astype(jnp.float32)[:, None] * inv_freq[None, :]
    cos = jnp.cos(ang)[:, None, :]
    sin = jnp.sin(ang)[:, None, :]
    zf = z.astype(jnp.float32)
    z1, z2 = zf[..., :HEAD_DIM // 2], zf[..., HEAD_DIM // 2:]
    out = jnp.concatenate([z1 * cos - z2 * sin, z2 * cos + z1 * sin], axis=-1)
    return out.astype(z.dtype)


def sliding_window_attention(q, k, v, sinks):
    b, t = q.shape[0], q.shape[1]
    nblk = t // BLOCK
    qb = q.reshape(b, nblk, BLOCK, N_KV_HEADS, GROUP, HEAD_DIM).transpose(1, 0, 2, 3, 4, 5)

    def band(z):
        zb = z.reshape(b, nblk, BLOCK, N_KV_HEADS, HEAD_DIM)
        prev = jnp.concatenate([jnp.zeros_like(zb[:, :1]), zb[:, :-1]], axis=1)
        return jnp.concatenate([prev, zb], axis=2).transpose(1, 0, 2, 3, 4)

    kb, vb = band(k), band(v)
    qi = jnp.arange(BLOCK)[:, None]
    kj = jnp.arange(2 * BLOCK)[None, :]
    in_band = (kj > qi) & (kj <= qi + BLOCK)
    sink = sinks.astype(jnp.float32).reshape(N_KV_HEADS, GROUP, 1, 1)
    scale = HEAD_DIM ** -0.5

    def one_block(args):
        blk, qblk, kblk, vblk = args
        logits = jnp.einsum('bqkgd,bskd->bkgqs', qblk, kblk).astype(jnp.float32) * scale
        valid = in_band & (kj >= BLOCK - blk * BLOCK)
        logits = jnp.where(valid, logits, -jnp.inf)
        m = jnp.maximum(jnp.max(logits, axis=-1, keepdims=True), sink)
        p = jnp.exp(logits - m)
        denom = jnp.sum(p, axis=-1, keepdims=True) + jnp.exp(sink - m)
        probs = (p / denom).astype(vblk.dtype)
        return jnp.einsum('bkgqs,bskd->bqkgd', probs, vblk)

    out = lax.map(one_block, (jnp.arange(nblk), qb, kb, vb))
    return out.transpose(1, 0, 2, 3, 4, 5).reshape(b, t, D_ATTN)


def hybrid_layer(x, g_pre, g_post, w_in, w_conv, sinks, w_proj_conv, w_proj_attn, w_out):
    b, t, _ = x.shape
    h = rmsnorm(x, g_pre)
    proj = jnp.einsum('btd,de->bte', h, w_in)
    xc, bg, cg, zc, q, k, v, za, ga, gb = jnp.split(proj, SPLIT_IDX, axis=-1)

    ua = jax.nn.silu(zc) * short_gated_conv(xc, bg, cg, w_conv)
    ya = jnp.einsum('btc,cd->btd', ua, w_proj_conv)

    positions = jnp.arange(t)
    q = rope(q.reshape(b, t, N_HEADS, HEAD_DIM), positions)
    k = rope(k.reshape(b, t, N_KV_HEADS, HEAD_DIM), positions)
    v = v.reshape(b, t, N_KV_HEADS, HEAD_DIM)
    ub = jax.nn.silu(za) * sliding_window_attention(q, k, v, sinks)
    yb = jnp.einsum('bta,ad->btd', ub, w_proj_attn)

    merged = jax.nn.sigmoid(ga) * ya + jax.nn.sigmoid(gb) * yb
    y = jnp.einsum('btd,de->bte', merged, w_out)
    return x + rmsnorm(y, g_post)


def setup_inputs(seed: int = 0) -> dict:
    key = jax.random.key(seed)
    ks = jax.random.split(key, 10)
    f32 = jnp.float32
    x = jax.random.normal(ks[0], (BATCH, SEQ, D_MODEL), f32)
    g_pre = 1.0 + 0.05 * jax.random.normal(ks[1], (DEPTH, D_MODEL), f32)
    g_post = 1.0 + 0.05 * jax.random.normal(ks[2], (DEPTH, D_MODEL), f32)
    w_in = jax.random.normal(ks[3], (DEPTH, D_MODEL, D_IN), f32) * D_MODEL ** -0.5
    w_conv = jax.random.normal(ks[4], (DEPTH, CONV_WIDTH, D_CONV), f32) * CONV_WIDTH ** -0.5
    sinks = 0.5 * jax.random.normal(ks[5], (DEPTH, N_HEADS), f32)
    w_proj_conv = jax.random.normal(ks[6], (DEPTH, D_CONV, D_MODEL), f32) * D_CONV ** -0.5
    w_proj_attn = jax.random.normal(ks[7], (DEPTH, D_ATTN, D_MODEL), f32) * D_ATTN ** -0.5
    w_out = jax.random.normal(ks[8], (DEPTH, D_MODEL, D_MODEL), f32) * D_MODEL ** -0.5
    return {"x": x, "g_pre": g_pre, "g_post": g_post, "w_in": w_in, "w_conv": w_conv,
            "sinks": sinks, "w_proj_conv": w_proj_conv, "w_proj_attn": w_proj_attn, "w_out": w_out}


def reference(x, g_pre, g_post, w_in, w_conv, sinks, w_proj_conv, w_proj_attn, w_out):
    for layer in range(DEPTH):
        x = hybrid_layer(x, g_pre[layer], g_post[layer], w_in[layer], w_conv[layer], sinks[layer],
                         w_proj_conv[layer], w_proj_attn[layer], w_out[layer])
    return x
```

```python
import functools

import jax
import jax.numpy as jnp
import numpy as np
from jax import lax
from jax.experimental import pallas as pl
from jax.experimental.pallas import tpu as pltpu

D_MODEL = 1024
D_CONV = D_MODEL
CONV_WIDTH = 3
HEAD_DIM = 64
N_HEADS = D_MODEL // HEAD_DIM
N_KV_HEADS = 2
GROUP = N_HEADS // N_KV_HEADS
D_ATTN = N_HEADS * HEAD_DIM
D_KV = N_KV_HEADS * HEAD_DIM
WINDOW = 128
BLOCK = WINDOW
ROPE_THETA = 10000.0
RMS_EPS = 1e-6

OFF_XC = 0
OFF_BG = OFF_XC + D_CONV
OFF_CG = OFF_BG + D_CONV
OFF_ZC = OFF_CG + D_CONV
OFF_Q = OFF_ZC + D_CONV
OFF_K = OFF_Q + D_ATTN
OFF_V = OFF_K + D_KV
OFF_ZA = OFF_V + D_KV
OFF_GA = OFF_ZA + D_ATTN
OFF_GB = OFF_GA + D_MODEL
D_IN = OFF_GB + D_MODEL

LANES = 128
SUBLANES = 8
PAIR = LANES // HEAD_DIM
N_PAIRS = N_HEADS // PAIR
PAIRS_PER_KV = GROUP // PAIR
TQ = 512
CC = 512
NEG = -1e30
VMEM_LIMIT = 60 * 1024 * 1024

assert D_KV == LANES and PAIR == 2 and TQ % BLOCK == 0 and D_CONV % CC == 0

_BF = jnp.bfloat16
_F32 = jnp.float32


def _dot(a, b):
    return jnp.dot(a, b, preferred_element_type=_F32)


def _dot_nt(a, b):
    return lax.dot_general(a, b, (((1,), (1,)), ((), ())), preferred_element_type=_F32)


def _sigmoid(z):
    return 1.0 / (1.0 + jnp.exp(-z))


def _layer_kernel(sink_ref, x_ref, gpre_ref, gpost_ref, wconv_ref, tab_ref,
                  win_ref, wpc_ref, wpa_ref, wout_ref, o_ref,
                  h_scr, ua_scr, ub_scr, q_scr, sza_scr, kv_scr, ucarry):
    j = pl.program_id(1)

    @pl.when(j == 0)
    def _():
        ucarry[...] = jnp.zeros_like(ucarry)
        kv_scr[:, 0:BLOCK, :] = jnp.zeros((8, BLOCK, LANES), _BF)

    x = x_ref[0]
    ms = jnp.mean(x * x, axis=-1, keepdims=True)
    h_scr[...] = (x * lax.rsqrt(ms + RMS_EPS) * gpre_ref[...]).astype(_BF)

    row = lax.broadcasted_iota(jnp.int32, (TQ, CC), 0)
    for c in range(D_CONV // CC):
        lo = c * CC
        hh = h_scr[...]
        xc = _dot(hh, win_ref[:, OFF_XC + lo:OFF_XC + lo + CC])
        cg = _dot(hh, win_ref[:, OFF_CG + lo:OFF_CG + lo + CC])
        u = cg * xc
        prev = ucarry[:, lo:lo + CC]
        p1 = prev[SUBLANES - 1:SUBLANES, :]
        p2 = prev[SUBLANES - 2:SUBLANES - 1, :]
        u1 = jnp.where(row == 0, p1, pltpu.roll(u, 1, 0))
        u2 = jnp.where(row == 0, p2, jnp.where(row == 1, p1, pltpu.roll(u, 2, 0)))
        ucarry[:, lo:lo + CC] = u[TQ - SUBLANES:, :]
        w = wconv_ref[:, lo:lo + CC]
        y = w[0:1, :] * u2 + w[1:2, :] * u1 + w[2:3, :] * u
        bg = _dot(hh, win_ref[:, OFF_BG + lo:OFF_BG + lo + CC])
        zc = _dot(hh, win_ref[:, OFF_ZC + lo:OFF_ZC + lo + CC])
        ua_scr[:, lo:lo + CC] = ((zc * _sigmoid(zc)) * (bg * y)).astype(_BF)

    lane = lax.broadcasted_iota(jnp.int32, (TQ, LANES), 1)
    first_half = (lane & (HEAD_DIM // 2)) == 0
    low_head = lane < HEAD_DIM

    def rope(z, cos, sin):
        rot = jnp.where(first_half, pltpu.roll(z, LANES - HEAD_DIM // 2, 1),
                        pltpu.roll(z, HEAD_DIM // 2, 1))
        return z * cos + rot * sin

    hh = h_scr[...]
    kv = _dot(hh, win_ref[:, OFF_K:OFF_K + 2 * D_KV])
    k = rope(kv[:, :D_KV], tab_ref[0], tab_ref[1])
    v = kv[:, D_KV:]
    kr = pltpu.roll(k, HEAD_DIM, 1)
    vr = pltpu.roll(v, HEAD_DIM, 1)
    zero = jnp.zeros_like(k)
    variants = (jnp.where(low_head, k, zero), jnp.where(low_head, zero, kr),
                jnp.where(low_head, kr, zero), jnp.where(low_head, zero, k),
                jnp.where(low_head, v, zero), jnp.where(low_head, zero, vr),
                jnp.where(low_head, vr, zero), jnp.where(low_head, zero, v))
    for i, a in enumerate(variants):
        kv_scr[i, BLOCK:BLOCK + TQ, :] = a.astype(_BF)

    qf = _dot(hh, win_ref[:, OFF_Q:OFF_Q + D_ATTN])
    for c in range(N_PAIRS):
        qc = rope(qf[:, c * LANES:(c + 1) * LANES], tab_ref[2], tab_ref[3])
        q_scr[:, c * LANES:(c + 1) * LANES] = qc.astype(_BF)
    za = _dot(hh, win_ref[:, OFF_ZA:OFF_ZA + D_ATTN])
    sza_scr[...] = za * _sigmoid(za)

    qi = lax.broadcasted_iota(jnp.int32, (BLOCK, 2 * BLOCK), 0)
    kj = lax.broadcasted_iota(jnp.int32, (BLOCK, 2 * BLOCK), 1)
    band = (kj > qi) & (kj <= qi + BLOCK)
    low_head_b = lax.broadcasted_iota(jnp.int32, (BLOCK, LANES), 1) < HEAD_DIM

    def softmax_parts(s, mask, sink):
        s = jnp.where(mask, s, NEG)
        m = jnp.maximum(jnp.max(s, axis=-1, keepdims=True), sink)
        p = jnp.exp(s - m)
        denom = jnp.sum(p, axis=-1, keepdims=True) + jnp.exp(sink - m)
        return p.astype(_BF), denom

    for b in range(TQ // BLOCK):
        r0 = b * BLOCK
        first_key = jnp.where(j == 0, BLOCK, 0) if b == 0 else 0
        mask = band & (kj >= first_key)
        for c in range(N_PAIRS):
            kvh = c // PAIRS_PER_KV
            qp = q_scr[r0:r0 + BLOCK, c * LANES:(c + 1) * LANES]
            ka = kv_scr[2 * kvh, r0:r0 + 2 * BLOCK, :]
            kb = kv_scr[2 * kvh + 1, r0:r0 + 2 * BLOCK, :]
            va = kv_scr[4 + 2 * kvh, r0:r0 + 2 * BLOCK, :]
            vb = kv_scr[4 + 2 * kvh + 1, r0:r0 + 2 * BLOCK, :]
            pa, da = softmax_parts(_dot_nt(qp, ka), mask, sink_ref[PAIR * c])
            pb, db = softmax_parts(_dot_nt(qp, kb), mask, sink_ref[PAIR * c + 1])
            o = _dot(pa, va) + _dot(pb, vb)
            o = o * jnp.where(low_head_b, 1.0 / da, 1.0 / db)
            gate = sza_scr[r0:r0 + BLOCK, c * LANES:(c + 1) * LANES]
            ub_scr[r0:r0 + BLOCK, c * LANES:(c + 1) * LANES] = (gate * o).astype(_BF)

    kv_scr[:, 0:BLOCK, :] = kv_scr[:, TQ:TQ + BLOCK, :]

    hh = h_scr[...]
    ya = _dot(ua_scr[...], wpc_ref[...])
    ga = _dot(hh, win_ref[:, OFF_GA:OFF_GA + D_MODEL])
    merged = _sigmoid(ga) * ya
    yb = _dot(ub_scr[...], wpa_ref[...])
    gb = _dot(hh, win_ref[:, OFF_GB:OFF_GB + D_MODEL])
    merged = (merged + _sigmoid(gb) * yb).astype(_BF)
    y = _dot(merged, wout_ref[...])
    ms2 = jnp.mean(y * y, axis=-1, keepdims=True)
    o_ref[0] = x_ref[0] + y * lax.rsqrt(ms2 + RMS_EPS) * gpost_ref[...]


def _rope_tables(t):
    inv_freq = ROPE_THETA ** (-jnp.arange(0, HEAD_DIM, 2, dtype=_F32) / HEAD_DIM)
    ang = jnp.arange(t).astype(_F32)[:, None] * inv_freq[None, :]
    cos = jnp.cos(ang)
    sin = jnp.sin(ang)
    cos = jnp.tile(jnp.concatenate([cos, cos], axis=-1), (1, PAIR))
    sin = jnp.tile(jnp.concatenate([-sin, sin], axis=-1), (1, PAIR))
    scale = HEAD_DIM ** -0.5
    return jnp.stack([cos, sin, cos * scale, sin * scale])


def _resident(shape):
    return pl.BlockSpec(shape, lambda b, j: (0,) * len(shape), pipeline_mode=pl.Buffered(1))


def _hybrid_layer(x, g_pre, g_post, w_in, w_conv, sinks, w_proj_conv, w_proj_attn, w_out):
    bsz, t, d = x.shape
    assert d == D_MODEL and t % TQ == 0 and w_in.shape == (D_MODEL, D_IN)
    tables = _rope_tables(t)
    return pl.pallas_call(
        _layer_kernel,
        out_shape=jax.ShapeDtypeStruct(x.shape, x.dtype),
        grid=(bsz, t // TQ),
        in_specs=[
            pl.BlockSpec(memory_space=pltpu.SMEM),
            pl.BlockSpec((1, TQ, D_MODEL), lambda b, j: (b, j, 0)),
            _resident((1, D_MODEL)),
            _resident((1, D_MODEL)),
            _resident((CONV_WIDTH, D_CONV)),
            pl.BlockSpec((4, TQ, LANES), lambda b, j: (0, j, 0)),
            _resident((D_MODEL, D_IN)),
            _resident((D_CONV, D_MODEL)),
            _resident((D_ATTN, D_MODEL)),
            _resident((D_MODEL, D_MODEL)),
        ],
        out_specs=pl.BlockSpec((1, TQ, D_MODEL), lambda b, j: (b, j, 0)),
        scratch_shapes=[
            pltpu.VMEM((TQ, D_MODEL), _BF),
            pltpu.VMEM((TQ, D_CONV), _BF),
            pltpu.VMEM((TQ, D_ATTN), _BF),
            pltpu.VMEM((TQ, D_ATTN), _BF),
            pltpu.VMEM((TQ, D_ATTN), _F32),
            pltpu.VMEM((8, BLOCK + TQ, LANES), _BF),
            pltpu.VMEM((SUBLANES, D_CONV), _F32),
        ],
        compiler_params=pltpu.CompilerParams(
            dimension_semantics=("arbitrary", "arbitrary"),
            vmem_limit_bytes=VMEM_LIMIT,
        ),
        name="hybrid_layer",
    )(sinks.astype(_F32), x, g_pre.reshape(1, D_MODEL), g_post.reshape(1, D_MODEL), w_conv,
      tables, w_in.astype(_BF), w_proj_conv.astype(_BF), w_proj_attn.astype(_BF),
      w_out.astype(_BF))


def kernel(x, g_pre, g_post, w_in, w_conv, sinks, w_proj_conv, w_proj_attn, w_out):
    for layer in range(g_pre.shape[0]):
        x = _hybrid_layer(x, g_pre[layer], g_post[layer], w_in[layer], w_conv[layer],
                          sinks[layer], w_proj_conv[layer], w_proj_attn[layer], w_out[layer])
    return x
```

```python
import jax
import jax.numpy as jnp
from jax import lax
from jax.experimental import pallas as pl
from jax.experimental.pallas import tpu as pltpu

D_MODEL = 1024
D_CONV = D_MODEL
CONV_WIDTH = 3
HEAD_DIM = 64
N_HEADS = D_MODEL // HEAD_DIM
N_KV_HEADS = 2
GROUP = N_HEADS // N_KV_HEADS
D_ATTN = N_HEADS * HEAD_DIM
D_KV = N_KV_HEADS * HEAD_DIM
WINDOW = 128
BLOCK = WINDOW
ROPE_THETA = 10000.0
RMS_EPS = 1e-6

OFF_XC = 0
OFF_BG = OFF_XC + D_CONV
OFF_CG = OFF_BG + D_CONV
OFF_ZC = OFF_CG + D_CONV
OFF_Q = OFF_ZC + D_CONV
OFF_K = OFF_Q + D_ATTN
OFF_V = OFF_K + D_KV
OFF_ZA = OFF_V + D_KV
OFF_GA = OFF_ZA + D_ATTN
OFF_GB = OFF_GA + D_MODEL
D_IN = OFF_GB + D_MODEL

LANES = 128
SUBLANES = 8
MXU_N = 256
PAIR = LANES // HEAD_DIM
N_PAIRS = N_HEADS // PAIR
PAIRS_PER_KV = GROUP // PAIR
TQ = 512
NEG = -1e30
VMEM_LIMIT = 60 * 1024 * 1024

assert D_KV == LANES and PAIR == 2 and TQ % BLOCK == 0 and D_MODEL % MXU_N == 0

_BF = jnp.bfloat16
_F32 = jnp.float32


def _dot(a, b):
    return jnp.dot(a, b, preferred_element_type=_F32)


def _dot_nt(a, b):
    return lax.dot_general(a, b, (((1,), (1,)), ((), ())), preferred_element_type=_F32)


def _sigmoid(z):
    return 1.0 / (1.0 + jnp.exp(-z))


def _layer_kernel(sink_ref, x_ref, gpre_ref, gpost_ref, wconv_ref, tab_ref,
                  win_ref, wpc_ref, wpa_ref, wout_ref, o_ref,
                  h_scr, ua_scr, ub_scr, q_scr, sza_scr, kv_scr, ucarry, t_scr, sgb_scr):
    j = pl.program_id(1)

    @pl.when(j == 0)
    def _():
        ucarry[...] = jnp.zeros_like(ucarry)
        kv_scr[:, 0:BLOCK, :] = jnp.zeros((8, BLOCK, LANES), _BF)

    x = x_ref[0]
    ms = jnp.mean(x * x, axis=-1, keepdims=True)
    h_scr[...] = (x * lax.rsqrt(ms + RMS_EPS) * gpre_ref[...]).astype(_BF)

    def proj(off, n):
        lo = off + n * MXU_N
        return _dot(h_scr[...], win_ref[:, lo:lo + MXU_N])

    def za_unit(n):
        za = proj(OFF_ZA, n)
        sza_scr[:, n * MXU_N:(n + 1) * MXU_N] = za * _sigmoid(za)

    row = lax.broadcasted_iota(jnp.int32, (TQ, MXU_N), 0)

    def dense_units():
        for n in range(1, D_ATTN // MXU_N):
            za_unit(n)
            yield
        for n in range(D_CONV // MXU_N):
            cols = slice(n * MXU_N, (n + 1) * MXU_N)
            xc = proj(OFF_XC, n)
            yield
            u = proj(OFF_CG, n) * xc
            prev = ucarry[:, cols]
            p1 = prev[SUBLANES - 1:SUBLANES, :]
            p2 = prev[SUBLANES - 2:SUBLANES - 1, :]
            u1 = jnp.where(row == 0, p1, pltpu.roll(u, 1, 0))
            u2 = jnp.where(row == 0, p2, jnp.where(row == 1, p1, pltpu.roll(u, 2, 0)))
            ucarry[:, cols] = u[TQ - SUBLANES:, :]
            w = wconv_ref[:, cols]
            y = w[0:1, :] * u2 + w[1:2, :] * u1 + w[2:3, :] * u
            yield
            y = proj(OFF_BG, n) * y
            yield
            zc = proj(OFF_ZC, n)
            ua_scr[:, cols] = ((zc * _sigmoid(zc)) * y).astype(_BF)
            yield
        for n in range(D_MODEL // MXU_N):
            cols = slice(n * MXU_N, (n + 1) * MXU_N)
            ya = _dot(ua_scr[...], wpc_ref[:, cols])
            yield
            t_scr[:, cols] = _sigmoid(proj(OFF_GA, n)) * ya
            yield
        for n in range(D_MODEL // MXU_N):
            sgb_scr[:, n * MXU_N:(n + 1) * MXU_N] = _sigmoid(proj(OFF_GB, n))
            yield

    lane = lax.broadcasted_iota(jnp.int32, (TQ, LANES), 1)
    first_half = (lane & (HEAD_DIM // 2)) == 0
    low_head = lane < HEAD_DIM

    def rope(z, cos, sin):
        rot = jnp.where(first_half, pltpu.roll(z, LANES - HEAD_DIM // 2, 1),
                        pltpu.roll(z, HEAD_DIM // 2, 1))
        return z * cos + rot * sin

    kv = _dot(h_scr[...], win_ref[:, OFF_K:OFF_K + 2 * D_KV])
    k = rope(kv[:, :D_KV], tab_ref[0], tab_ref[1])
    v = kv[:, D_KV:]
    kr = pltpu.roll(k, HEAD_DIM, 1)
    vr = pltpu.roll(v, HEAD_DIM, 1)
    zero = jnp.zeros_like(k)
    variants = (jnp.where(low_head, k, zero), jnp.where(low_head, zero, kr),
                jnp.where(low_head, kr, zero), jnp.where(low_head, zero, k),
                jnp.where(low_head, v, zero), jnp.where(low_head, zero, vr),
                jnp.where(low_head, vr, zero), jnp.where(low_head, zero, v))
    for i, a in enumerate(variants):
        kv_scr[i, BLOCK:BLOCK + TQ, :] = a.astype(_BF)

    for n in range(D_ATTN // MXU_N):
        qf = proj(OFF_Q, n)
        for c in range(MXU_N // LANES):
            qc = rope(qf[:, c * LANES:(c + 1) * LANES], tab_ref[2], tab_ref[3])
            lo = n * MXU_N + c * LANES
            q_scr[:, lo:lo + LANES] = qc.astype(_BF)
    za_unit(0)

    qi = lax.broadcasted_iota(jnp.int32, (BLOCK, 2 * BLOCK), 0)
    kj = lax.broadcasted_iota(jnp.int32, (BLOCK, 2 * BLOCK), 1)
    band = (kj > qi) & (kj <= qi + BLOCK)
    low_head_b = lax.broadcasted_iota(jnp.int32, (BLOCK, LANES), 1) < HEAD_DIM

    def softmax_parts(s, mask, sink):
        s = jnp.where(mask, s, NEG)
        m = jnp.maximum(jnp.max(s, axis=-1, keepdims=True), sink)
        p = jnp.exp(s - m)
        denom = jnp.sum(p, axis=-1, keepdims=True) + jnp.exp(sink - m)
        return p.astype(_BF), denom

    dense = dense_units()
    for c in range(N_PAIRS):
        kvh = c // PAIRS_PER_KV
        for b in range(TQ // BLOCK):
            r0 = b * BLOCK
            first_key = jnp.where(j == 0, BLOCK, 0) if b == 0 else 0
            mask = band & (kj >= first_key)
            qp = q_scr[r0:r0 + BLOCK, c * LANES:(c + 1) * LANES]
            ka = kv_scr[2 * kvh, r0:r0 + 2 * BLOCK, :]
            kb = kv_scr[2 * kvh + 1, r0:r0 + 2 * BLOCK, :]
            va = kv_scr[4 + 2 * kvh, r0:r0 + 2 * BLOCK, :]
            vb = kv_scr[4 + 2 * kvh + 1, r0:r0 + 2 * BLOCK, :]
            sa = _dot_nt(qp, ka)
            sb = _dot_nt(qp, kb)
            next(dense, None)
            pa, da = softmax_parts(sa, mask, sink_ref[PAIR * c])
            pb, db = softmax_parts(sb, mask, sink_ref[PAIR * c + 1])
            o = _dot(pa, va) + _dot(pb, vb)
            o = o * jnp.where(low_head_b, 1.0 / da, 1.0 / db)
            gate = sza_scr[r0:r0 + BLOCK, c * LANES:(c + 1) * LANES]
            ub_scr[r0:r0 + BLOCK, c * LANES:(c + 1) * LANES] = (gate * o).astype(_BF)
    for _ in dense:
        pass

    kv_scr[:, 0:BLOCK, :] = kv_scr[:, TQ:TQ + BLOCK, :]

    for n in range(D_MODEL // MXU_N):
        cols = slice(n * MXU_N, (n + 1) * MXU_N)
        yb = _dot(ub_scr[...], wpa_ref[:, cols])
        ua_scr[:, cols] = (t_scr[:, cols] + sgb_scr[:, cols] * yb).astype(_BF)
    y = _dot(ua_scr[...], wout_ref[...])
    ms2 = jnp.mean(y * y, axis=-1, keepdims=True)
    o_ref[0] = x_ref[0] + y * lax.rsqrt(ms2 + RMS_EPS) * gpost_ref[...]


def _rope_tables(t):
    inv_freq = ROPE_THETA ** (-jnp.arange(0, HEAD_DIM, 2, dtype=_F32) / HEAD_DIM)
    ang = jnp.arange(t).astype(_F32)[:, None] * inv_freq[None, :]
    cos = jnp.cos(ang)
    sin = jnp.sin(ang)
    cos = jnp.tile(jnp.concatenate([cos, cos], axis=-1), (1, PAIR))
    sin = jnp.tile(jnp.concatenate([-sin, sin], axis=-1), (1, PAIR))
    scale = HEAD_DIM ** -0.5
    return jnp.stack([cos, sin, cos * scale, sin * scale])


def _resident(shape):
    return pl.BlockSpec(shape, lambda b, j: (0,) * len(shape), pipeline_mode=pl.Buffered(1))


def _hybrid_layer(x, g_pre, g_post, w_in, w_conv, sinks, w_proj_conv, w_proj_attn, w_out):
    bsz, t, d = x.shape
    assert d == D_MODEL and t % TQ == 0 and w_in.shape == (D_MODEL, D_IN)
    tables = _rope_tables(t)
    return pl.pallas_call(
        _layer_kernel,
        out_shape=jax.ShapeDtypeStruct(x.shape, x.dtype),
        grid=(bsz, t // TQ),
        in_specs=[
            pl.BlockSpec(memory_space=pltpu.SMEM),
            pl.BlockSpec((1, TQ, D_MODEL), lambda b, j: (b, j, 0)),
            _resident((1, D_MODEL)),
            _resident((1, D_MODEL)),
            _resident((CONV_WIDTH, D_CONV)),
            pl.BlockSpec((4, TQ, LANES), lambda b, j: (0, j, 0)),
            _resident((D_MODEL, D_IN)),
            _resident((D_CONV, D_MODEL)),
            _resident((D_ATTN, D_MODEL)),
            _resident((D_MODEL, D_MODEL)),
        ],
        out_specs=pl.BlockSpec((1, TQ, D_MODEL), lambda b, j: (b, j, 0)),
        scratch_shapes=[
            pltpu.VMEM((TQ, D_MODEL), _BF),
            pltpu.VMEM((TQ, D_CONV), _BF),
            pltpu.VMEM((TQ, D_ATTN), _BF),
            pltpu.VMEM((TQ, D_ATTN), _BF),
            pltpu.VMEM((TQ, D_ATTN), _F32),
            pltpu.VMEM((8, BLOCK + TQ, LANES), _BF),
            pltpu.VMEM((SUBLANES, D_CONV), _F32),
            pltpu.VMEM((TQ, D_MODEL), _F32),
            pltpu.VMEM((TQ, D_MODEL), _F32),
        ],
        compiler_params=pltpu.CompilerParams(
            dimension_semantics=("arbitrary", "arbitrary"),
            vmem_limit_bytes=VMEM_LIMIT,
        ),
        name="hybrid_layer",
    )(sinks.astype(_F32), x, g_pre.reshape(1, D_MODEL), g_post.reshape(1, D_MODEL), w_conv,
      tables, w_in.astype(_BF), w_proj_conv.astype(_BF), w_proj_attn.astype(_BF),
      w_out.astype(_BF))


def kernel(x, g_pre, g_post, w_in, w_conv, sinks, w_proj_conv, w_proj_attn, w_out):
    for layer in range(g_pre.shape[0]):
        x = _hybrid_layer(x, g_pre[layer], g_post[layer], w_in[layer], w_conv[layer],
                          sinks[layer], w_proj_conv[layer], w_proj_attn[layer], w_out[layer])
    return x
```

```python
import functools

import jax
import jax.numpy as jnp
from jax import lax
from jax.experimental import pallas as pl
from jax.experimental.pallas import tpu as pltpu

D_MODEL = 1024
D_CONV = D_MODEL
CONV_WIDTH = 3
HEAD_DIM = 64
N_HEADS = D_MODEL // HEAD_DIM
N_KV_HEADS = 2
GROUP = N_HEADS // N_KV_HEADS
D_ATTN = N_HEADS * HEAD_DIM
D_KV = N_KV_HEADS * HEAD_DIM
WINDOW = 128
BLOCK = WINDOW
ROPE_THETA = 10000.0
RMS_EPS = 1e-6

OFF_XC = 0
OFF_BG = OFF_XC + D_CONV
OFF_CG = OFF_BG + D_CONV
OFF_ZC = OFF_CG + D_CONV
OFF_Q = OFF_ZC + D_CONV
OFF_K = OFF_Q + D_ATTN
OFF_V = OFF_K + D_KV
OFF_ZA = OFF_V + D_KV
OFF_GA = OFF_ZA + D_ATTN
OFF_GB = OFF_GA + D_MODEL
D_IN = OFF_GB + D_MODEL

LANES = 128
SUBLANES = 8
MXU_N = 256
PAIR = LANES // HEAD_DIM
N_PAIRS = N_HEADS // PAIR
PAIRS_PER_KV = GROUP // PAIR
TQ = 512
NEG = -1e30
VMEM_LIMIT = 62 * 1024 * 1024

assert D_KV == LANES and PAIR == 2 and TQ % BLOCK == 0 and D_MODEL % MXU_N == 0
assert D_CONV == D_ATTN == D_MODEL

_BF = jnp.bfloat16
_F32 = jnp.float32


def _dot(a, b):
    return jnp.dot(a, b, preferred_element_type=_F32)


def _dot_nt(a, b):
    return lax.dot_general(a, b, (((1,), (1,)), ((), ())), preferred_element_type=_F32)


def _sigmoid(z):
    return 1.0 / (1.0 + jnp.exp(-z))


def _layer_kernel(n_tiles, tiles_per_seq, sink_ref, x_ref, gpre_ref, gpost_ref, wconv_ref,
                  tab_ref, win_ref, wpc_ref, wpa_ref, wout_ref, o_ref,
                  h_scr, ua_scr, ub_scr, q_scr, sza_scr, k_scr, v_scr, ucarry, t_scr, sgb_scr,
                  x_scr):
    s = pl.program_id(0)
    j = lax.rem(jnp.minimum(s, n_tiles - 1), tiles_per_seq)

    @pl.when(s == 0)
    def _():
        ua_scr[...] = jnp.zeros_like(ua_scr)
        x_scr[...] = jnp.zeros_like(x_scr)
        low = lax.broadcasted_iota(jnp.int32, (BLOCK + TQ, LANES), 1) < HEAD_DIM
        ones_lo = jnp.where(low, 1.0, 0.0).astype(_BF)
        ones_hi = jnp.where(low, 0.0, 1.0).astype(_BF)
        for i in range(2 * N_KV_HEADS):
            v_scr[i, :, LANES:] = ones_hi if i % 2 else ones_lo

    @pl.when(j == 0)
    def _():
        ucarry[...] = jnp.zeros_like(ucarry)
        k_scr[:, 0:BLOCK, :] = jnp.zeros((2 * N_KV_HEADS, BLOCK, LANES), _BF)
        v_scr[:, 0:BLOCK, 0:LANES] = jnp.zeros((2 * N_KV_HEADS, BLOCK, LANES), _BF)

    @pl.when(j != 0)
    def _():
        k_scr[:, 0:BLOCK, :] = k_scr[:, TQ:TQ + BLOCK, :]
        v_scr[:, 0:BLOCK, 0:LANES] = v_scr[:, TQ:TQ + BLOCK, 0:LANES]

    x = x_ref[0]
    ms = jnp.mean(x * x, axis=-1, keepdims=True)
    h_scr[...] = (x * lax.rsqrt(ms + RMS_EPS) * gpre_ref[...]).astype(_BF)

    for n in range(D_MODEL // MXU_N):
        cols = slice(n * MXU_N, (n + 1) * MXU_N)
        t_scr[:, cols] = _dot(ua_scr[...], wout_ref[:, cols])

    def proj(off, n):
        lo = off + n * MXU_N
        return _dot(h_scr[...], win_ref[:, lo:lo + MXU_N])

    lane = lax.broadcasted_iota(jnp.int32, (TQ, LANES), 1)
    first_half = (lane & (HEAD_DIM // 2)) == 0
    low_head = lane < HEAD_DIM

    def rope(z, cos, sin):
        rot = jnp.where(first_half, pltpu.roll(z, LANES - HEAD_DIM // 2, 1),
                        pltpu.roll(z, HEAD_DIM // 2, 1))
        return z * cos + rot * sin

    def q_unit(n):
        qf = proj(OFF_Q, n) * (HEAD_DIM ** -0.5)
        for c in range(MXU_N // LANES):
            qc = rope(qf[:, c * LANES:(c + 1) * LANES], tab_ref[0], tab_ref[1])
            lo = n * MXU_N + c * LANES
            q_scr[:, lo:lo + LANES] = qc.astype(_BF)

    def za_unit(n):
        za = proj(OFF_ZA, n)
        sza_scr[:, n * MXU_N:(n + 1) * MXU_N] = za * _sigmoid(za)

    row = lax.broadcasted_iota(jnp.int32, (TQ, MXU_N), 0)

    def dense_units():
        y = t_scr[...]
        ms2 = jnp.mean(y * y, axis=-1, keepdims=True)
        o_ref[0] = x_scr[...] + y * lax.rsqrt(ms2 + RMS_EPS) * gpost_ref[...]
        x_scr[...] = x_ref[0]
        for n in range(1, D_ATTN // MXU_N):
            q_unit(n)
            yield
            za_unit(n)
            yield
        for n in range(D_MODEL // MXU_N):
            cols = slice(n * MXU_N, (n + 1) * MXU_N)
            sgb_scr[:, cols] = _sigmoid(proj(OFF_GB, n))
            yield
            t_scr[:, cols] = _sigmoid(proj(OFF_GA, n))
            yield
        for n in range(D_CONV // MXU_N):
            cols = slice(n * MXU_N, (n + 1) * MXU_N)
            xc = proj(OFF_XC, n)
            yield
            u = proj(OFF_CG, n) * xc
            prev = ucarry[:, cols]
            p1 = prev[SUBLANES - 1:SUBLANES, :]
            p2 = prev[SUBLANES - 2:SUBLANES - 1, :]
            u1 = jnp.where(row == 0, p1, pltpu.roll(u, 1, 0))
            u2 = jnp.where(row == 0, p2, jnp.where(row == 1, p1, pltpu.roll(u, 2, 0)))
            ucarry[:, cols] = u[TQ - SUBLANES:, :]
            w = wconv_ref[:, cols]
            y = w[0:1, :] * u2 + w[1:2, :] * u1 + w[2:3, :] * u
            yield
            y = proj(OFF_BG, n) * y
            yield
            zc = proj(OFF_ZC, n)
            ua_scr[:, cols] = ((zc * _sigmoid(zc)) * y).astype(_BF)
            yield

    kv = _dot(h_scr[...], win_ref[:, OFF_K:OFF_K + 2 * D_KV])
    k = rope(kv[:, :D_KV], tab_ref[0], tab_ref[1])
    v = kv[:, D_KV:]
    kr = pltpu.roll(k, HEAD_DIM, 1)
    vr = pltpu.roll(v, HEAD_DIM, 1)
    zero = jnp.zeros_like(k)
    k_variants = (jnp.where(low_head, k, zero), jnp.where(low_head, zero, kr),
                  jnp.where(low_head, kr, zero), jnp.where(low_head, zero, k))
    v_variants = (jnp.where(low_head, v, zero), jnp.where(low_head, zero, vr),
                  jnp.where(low_head, vr, zero), jnp.where(low_head, zero, v))
    for i in range(2 * N_KV_HEADS):
        k_scr[i, BLOCK:BLOCK + TQ, :] = k_variants[i].astype(_BF)
        v_scr[i, BLOCK:BLOCK + TQ, 0:LANES] = v_variants[i].astype(_BF)
    q_unit(0)
    za_unit(0)

    qi = lax.broadcasted_iota(jnp.int32, (BLOCK, 2 * BLOCK), 0)
    kj = lax.broadcasted_iota(jnp.int32, (BLOCK, 2 * BLOCK), 1)
    band = (kj > qi) & (kj <= qi + BLOCK)
    low_head_b = lax.broadcasted_iota(jnp.int32, (BLOCK, LANES), 1) < HEAD_DIM

    def softmax_parts(sc, mask, sink):
        sc = jnp.where(mask, sc, NEG)
        m = jnp.maximum(jnp.max(sc, axis=-1, keepdims=True), sink)
        return jnp.exp(sc - m).astype(_BF), jnp.exp(sink - m)

    dense = dense_units()
    for c in range(N_PAIRS):
        kvh = c // PAIRS_PER_KV
        for b in range(TQ // BLOCK):
            r0 = b * BLOCK
            first_key = jnp.where(j == 0, BLOCK, 0) if b == 0 else 0
            mask = band & (kj >= first_key)
            qp = q_scr[r0:r0 + BLOCK, c * LANES:(c + 1) * LANES]
            ka = k_scr[2 * kvh, r0:r0 + 2 * BLOCK, :]
            kb = k_scr[2 * kvh + 1, r0:r0 + 2 * BLOCK, :]
            va = v_scr[2 * kvh, r0:r0 + 2 * BLOCK, :]
            vb = v_scr[2 * kvh + 1, r0:r0 + 2 * BLOCK, :]
            sa = _dot_nt(qp, ka)
            sb = _dot_nt(qp, kb)
            next(dense, None)
            pa, ea = softmax_parts(sa, mask, sink_ref[PAIR * c])
            pb, eb = softmax_parts(sb, mask, sink_ref[PAIR * c + 1])
            ol = _dot(pa, va) + _dot(pb, vb)
            denom = ol[:, LANES:] + jnp.where(low_head_b, ea, eb)
            gate = sza_scr[r0:r0 + BLOCK, c * LANES:(c + 1) * LANES]
            ub_scr[r0:r0 + BLOCK, c * LANES:(c + 1) * LANES] = (
                gate * (ol[:, :LANES] * (1.0 / denom))).astype(_BF)
    for _ in dense:
        pass

    for n in range(D_MODEL // MXU_N):
        cols = slice(n * MXU_N, (n + 1) * MXU_N)
        t_scr[:, cols] = t_scr[:, cols] * _dot(ua_scr[...], wpc_ref[:, cols])
    for n in range(D_MODEL // MXU_N):
        cols = slice(n * MXU_N, (n + 1) * MXU_N)
        yb = _dot(ub_scr[...], wpa_ref[:, cols])
        ua_scr[:, cols] = (t_scr[:, cols] + sgb_scr[:, cols] * yb).astype(_BF)


def _rope_tables(t):
    inv_freq = ROPE_THETA ** (-jnp.arange(0, HEAD_DIM, 2, dtype=_F32) / HEAD_DIM)
    ang = jnp.arange(t).astype(_F32)[:, None] * inv_freq[None, :]
    cos = jnp.cos(ang)
    sin = jnp.sin(ang)
    cos = jnp.tile(jnp.concatenate([cos, cos], axis=-1), (1, PAIR))
    sin = jnp.tile(jnp.concatenate([-sin, sin], axis=-1), (1, PAIR))
    return jnp.stack([cos, sin])


def _resident(shape):
    return pl.BlockSpec(shape, lambda s: (0,) * len(shape), pipeline_mode=pl.Buffered(1))


def _hybrid_layer(x, g_pre, g_post, w_in, w_conv, sinks, w_proj_conv, w_proj_attn, w_out):
    bsz, t, d = x.shape
    assert d == D_MODEL and t % TQ == 0 and w_in.shape == (D_MODEL, D_IN)
    nt = t // TQ
    n_tiles = bsz * nt

    def cur(s):
        tile = jnp.minimum(s, n_tiles - 1)
        return tile // nt, tile % nt

    def prev(s):
        tile = jnp.maximum(s - 1, 0)
        return tile // nt, tile % nt

    return pl.pallas_call(
        functools.partial(_layer_kernel, n_tiles, nt),
        out_shape=jax.ShapeDtypeStruct(x.shape, x.dtype),
        grid=(n_tiles + 1,),
        in_specs=[
            pl.BlockSpec(memory_space=pltpu.SMEM),
            pl.BlockSpec((1, TQ, D_MODEL), lambda s: (*cur(s), 0)),
            _resident((1, D_MODEL)),
            _resident((1, D_MODEL)),
            _resident((CONV_WIDTH, D_CONV)),
            pl.BlockSpec((2, TQ, LANES), lambda s: (0, cur(s)[1], 0)),
            _resident((D_MODEL, D_IN)),
            _resident((D_CONV, D_MODEL)),
            _resident((D_ATTN, D_MODEL)),
            _resident((D_MODEL, D_MODEL)),
        ],
        out_specs=pl.BlockSpec((1, TQ, D_MODEL), lambda s: (*prev(s), 0)),
        scratch_shapes=[
            pltpu.VMEM((TQ, D_MODEL), _BF),
            pltpu.VMEM((TQ, D_CONV), _BF),
            pltpu.VMEM((TQ, D_ATTN), _BF),
            pltpu.VMEM((TQ, D_ATTN), _BF),
            pltpu.VMEM((TQ, D_ATTN), _F32),
            pltpu.VMEM((2 * N_KV_HEADS, BLOCK + TQ, LANES), _BF),
            pltpu.VMEM((2 * N_KV_HEADS, BLOCK + TQ, 2 * LANES), _BF),
            pltpu.VMEM((SUBLANES, D_CONV), _F32),
            pltpu.VMEM((TQ, D_MODEL), _F32),
            pltpu.VMEM((TQ, D_MODEL), _F32),
            pltpu.VMEM((TQ, D_MODEL), _F32),
        ],
        compiler_params=pltpu.CompilerParams(
            dimension_semantics=("arbitrary",),
            vmem_limit_bytes=VMEM_LIMIT,
        ),
        name="hybrid_layer",
    )(sinks.astype(_F32), x, g_pre.reshape(1, D_MODEL), g_post.reshape(1, D_MODEL), w_conv,
      _rope_tables(t), w_in.astype(_BF), w_proj_conv.astype(_BF), w_proj_attn.astype(_BF),
      w_out.astype(_BF))


def kernel(x, g_pre, g_post, w_in, w_conv, sinks, w_proj_conv, w_proj_attn, w_out):
    for layer in range(g_pre.shape[0]):
        x = _hybrid_layer(x, g_pre[layer], g_post[layer], w_in[layer], w_conv[layer],
                          sinks[layer], w_proj_conv[layer], w_proj_attn[layer], w_out[layer])
    return x
```

```python
import jax
import jax.numpy as jnp
from jax import lax
from jax.experimental import pallas as pl
from jax.experimental.pallas import tpu as pltpu

D_MODEL = 1024
D_CONV = D_MODEL
CONV_WIDTH = 3
HEAD_DIM = 64
N_HEADS = D_MODEL // HEAD_DIM
N_KV_HEADS = 2
GROUP = N_HEADS // N_KV_HEADS
D_ATTN = N_HEADS * HEAD_DIM
D_KV = N_KV_HEADS * HEAD_DIM
WINDOW = 128
BLOCK = WINDOW
ROPE_THETA = 10000.0
RMS_EPS = 1e-6

OFF_XC = 0
OFF_BG = OFF_XC + D_CONV
OFF_CG = OFF_BG + D_CONV
OFF_ZC = OFF_CG + D_CONV
OFF_Q = OFF_ZC + D_CONV
OFF_K = OFF_Q + D_ATTN
OFF_V = OFF_K + D_KV
OFF_ZA = OFF_V + D_KV
OFF_GA = OFF_ZA + D_ATTN
OFF_GB = OFF_GA + D_MODEL
D_IN = OFF_GB + D_MODEL

LANES = 128
SUBLANES = 8
MXU_N = 256
PAIR = LANES // HEAD_DIM
N_PAIRS = N_HEADS // PAIR
PAIRS_PER_KV = GROUP // PAIR
TQ = 512
NEG = -1e30
VMEM_LIMIT = 60 * 1024 * 1024

assert D_KV == LANES and PAIR == 2 and TQ % BLOCK == 0 and D_MODEL % MXU_N == 0

_BF = jnp.bfloat16
_F32 = jnp.float32


def _dot(a, b):
    return jnp.dot(a, b, preferred_element_type=_F32)


def _dot_nt(a, b):
    return lax.dot_general(a, b, (((1,), (1,)), ((), ())), preferred_element_type=_F32)


def _sigmoid(z):
    return 1.0 / (1.0 + jnp.exp(-z))


def _layer_kernel(sink_ref, x_ref, gpre_ref, gpost_ref, wconv_ref, tab_ref,
                  win_ref, wpc_ref, wpa_ref, wout_ref, o_ref,
                  h_scr, ua_scr, ub_scr, q_scr, sza_scr, k_scr, v_scr, ucarry, t_scr, sgb_scr):
    j = pl.program_id(1)

    @pl.when((pl.program_id(0) == 0) & (j == 0))
    def _():
        low = lax.broadcasted_iota(jnp.int32, (BLOCK + TQ, LANES), 1) < HEAD_DIM
        ones_lo = jnp.where(low, 1.0, 0.0).astype(_BF)
        ones_hi = jnp.where(low, 0.0, 1.0).astype(_BF)
        for i in range(2 * N_KV_HEADS):
            v_scr[i, :, LANES:] = ones_hi if i % 2 else ones_lo

    @pl.when(j == 0)
    def _():
        ucarry[...] = jnp.zeros_like(ucarry)
        k_scr[:, 0:BLOCK, :] = jnp.zeros((2 * N_KV_HEADS, BLOCK, LANES), _BF)
        v_scr[:, 0:BLOCK, 0:LANES] = jnp.zeros((2 * N_KV_HEADS, BLOCK, LANES), _BF)

    @pl.when(j != 0)
    def _():
        k_scr[:, 0:BLOCK, :] = k_scr[:, TQ:TQ + BLOCK, :]
        v_scr[:, 0:BLOCK, 0:LANES] = v_scr[:, TQ:TQ + BLOCK, 0:LANES]

    x = x_ref[0]
    ms = jnp.mean(x * x, axis=-1, keepdims=True)
    h_scr[...] = (x * lax.rsqrt(ms + RMS_EPS) * gpre_ref[...]).astype(_BF)

    def proj(off, n):
        lo = off + n * MXU_N
        return _dot(h_scr[...], win_ref[:, lo:lo + MXU_N])

    lane = lax.broadcasted_iota(jnp.int32, (TQ, LANES), 1)
    first_half = (lane & (HEAD_DIM // 2)) == 0
    low_head = lane < HEAD_DIM

    def rope(z, cos, sin):
        rot = jnp.where(first_half, pltpu.roll(z, LANES - HEAD_DIM // 2, 1),
                        pltpu.roll(z, HEAD_DIM // 2, 1))
        return z * cos + rot * sin

    def q_unit(n):
        qf = proj(OFF_Q, n)
        for c in range(MXU_N // LANES):
            qc = rope(qf[:, c * LANES:(c + 1) * LANES], tab_ref[2], tab_ref[3])
            lo = n * MXU_N + c * LANES
            q_scr[:, lo:lo + LANES] = qc.astype(_BF)

    def za_unit(n):
        za = proj(OFF_ZA, n)
        sza_scr[:, n * MXU_N:(n + 1) * MXU_N] = za * _sigmoid(za)

    row = lax.broadcasted_iota(jnp.int32, (TQ, MXU_N), 0)

    def dense_units():
        for n in range(1, D_ATTN // MXU_N):
            q_unit(n)
            yield
            za_unit(n)
            yield
        for n in range(D_CONV // MXU_N):
            cols = slice(n * MXU_N, (n + 1) * MXU_N)
            xc = proj(OFF_XC, n)
            yield
            u = proj(OFF_CG, n) * xc
            prev = ucarry[:, cols]
            p1 = prev[SUBLANES - 1:SUBLANES, :]
            p2 = prev[SUBLANES - 2:SUBLANES - 1, :]
            u1 = jnp.where(row == 0, p1, pltpu.roll(u, 1, 0))
            u2 = jnp.where(row == 0, p2, jnp.where(row == 1, p1, pltpu.roll(u, 2, 0)))
            ucarry[:, cols] = u[TQ - SUBLANES:, :]
            w = wconv_ref[:, cols]
            y = w[0:1, :] * u2 + w[1:2, :] * u1 + w[2:3, :] * u
            yield
            y = proj(OFF_BG, n) * y
            yield
            zc = proj(OFF_ZC, n)
            ua_scr[:, cols] = ((zc * _sigmoid(zc)) * y).astype(_BF)
            yield
        for n in range(D_MODEL // MXU_N):
            cols = slice(n * MXU_N, (n + 1) * MXU_N)
            ya = _dot(ua_scr[...], wpc_ref[:, cols])
            yield
            t_scr[:, cols] = _sigmoid(proj(OFF_GA, n)) * ya
            yield
        for n in range(D_MODEL // MXU_N):
            sgb_scr[:, n * MXU_N:(n + 1) * MXU_N] = _sigmoid(proj(OFF_GB, n))
            yield

    kv = _dot(h_scr[...], win_ref[:, OFF_K:OFF_K + 2 * D_KV])
    k = rope(kv[:, :D_KV], tab_ref[0], tab_ref[1])
    v = kv[:, D_KV:]
    kr = pltpu.roll(k, HEAD_DIM, 1)
    vr = pltpu.roll(v, HEAD_DIM, 1)
    zero = jnp.zeros_like(k)
    k_variants = (jnp.where(low_head, k, zero), jnp.where(low_head, zero, kr),
                  jnp.where(low_head, kr, zero), jnp.where(low_head, zero, k))
    v_variants = (jnp.where(low_head, v, zero), jnp.where(low_head, zero, vr),
                  jnp.where(low_head, vr, zero), jnp.where(low_head, zero, v))
    for i in range(2 * N_KV_HEADS):
        k_scr[i, BLOCK:BLOCK + TQ, :] = k_variants[i].astype(_BF)
        v_scr[i, BLOCK:BLOCK + TQ, 0:LANES] = v_variants[i].astype(_BF)
    q_unit(0)
    za_unit(0)

    qi = lax.broadcasted_iota(jnp.int32, (BLOCK, 2 * BLOCK), 0)
    kj = lax.broadcasted_iota(jnp.int32, (BLOCK, 2 * BLOCK), 1)
    band = (kj > qi) & (kj <= qi + BLOCK)
    low_head_b = lax.broadcasted_iota(jnp.int32, (BLOCK, LANES), 1) < HEAD_DIM

    def softmax_parts(sc, mask, sink):
        sc = jnp.where(mask, sc, NEG)
        m = jnp.maximum(jnp.max(sc, axis=-1, keepdims=True), sink)
        return jnp.exp(sc - m).astype(_BF), jnp.exp(sink - m)

    dense = dense_units()
    for c in range(N_PAIRS):
        kvh = c // PAIRS_PER_KV
        for b in range(TQ // BLOCK):
            r0 = b * BLOCK
            first_key = jnp.where(j == 0, BLOCK, 0) if b == 0 else 0
            mask = band & (kj >= first_key)
            qp = q_scr[r0:r0 + BLOCK, c * LANES:(c + 1) * LANES]
            ka = k_scr[2 * kvh, r0:r0 + 2 * BLOCK, :]
            kb = k_scr[2 * kvh + 1, r0:r0 + 2 * BLOCK, :]
            va = v_scr[2 * kvh, r0:r0 + 2 * BLOCK, :]
            vb = v_scr[2 * kvh + 1, r0:r0 + 2 * BLOCK, :]
            sa = _dot_nt(qp, ka)
            sb = _dot_nt(qp, kb)
            next(dense, None)
            pa, ea = softmax_parts(sa, mask, sink_ref[PAIR * c])
            pb, eb = softmax_parts(sb, mask, sink_ref[PAIR * c + 1])
            ol = _dot(pa, va) + _dot(pb, vb)
            denom = ol[:, LANES:] + jnp.where(low_head_b, ea, eb)
            gate = sza_scr[r0:r0 + BLOCK, c * LANES:(c + 1) * LANES]
            ub_scr[r0:r0 + BLOCK, c * LANES:(c + 1) * LANES] = (
                gate * (ol[:, :LANES] * (1.0 / denom))).astype(_BF)
    for _ in dense:
        pass

    for n in range(D_MODEL // MXU_N):
        cols = slice(n * MXU_N, (n + 1) * MXU_N)
        yb = _dot(ub_scr[...], wpa_ref[:, cols])
        ua_scr[:, cols] = (t_scr[:, cols] + sgb_scr[:, cols] * yb).astype(_BF)
    y = _dot(ua_scr[...], wout_ref[...])
    ms2 = jnp.mean(y * y, axis=-1, keepdims=True)
    o_ref[0] = x_ref[0] + y * lax.rsqrt(ms2 + RMS_EPS) * gpost_ref[...]


def _rope_tables(t):
    inv_freq = ROPE_THETA ** (-jnp.arange(0, HEAD_DIM, 2, dtype=_F32) / HEAD_DIM)
    ang = jnp.arange(t).astype(_F32)[:, None] * inv_freq[None, :]
    cos = jnp.cos(ang)
    sin = jnp.sin(ang)
    cos = jnp.tile(jnp.concatenate([cos, cos], axis=-1), (1, PAIR))
    sin = jnp.tile(jnp.concatenate([-sin, sin], axis=-1), (1, PAIR))
    scale = HEAD_DIM ** -0.5
    return jnp.stack([cos, sin, cos * scale, sin * scale])


def _resident(shape):
    return pl.BlockSpec(shape, lambda b, j: (0,) * len(shape), pipeline_mode=pl.Buffered(1))


def _hybrid_layer(x, g_pre, g_post, w_in, w_conv, sinks, w_proj_conv, w_proj_attn, w_out):
    bsz, t, d = x.shape
    assert d == D_MODEL and t % TQ == 0 and w_in.shape == (D_MODEL, D_IN)
    tables = _rope_tables(t)
    return pl.pallas_call(
        _layer_kernel,
        out_shape=jax.ShapeDtypeStruct(x.shape, x.dtype),
        grid=(bsz, t // TQ),
        in_specs=[
            pl.BlockSpec(memory_space=pltpu.SMEM),
            pl.BlockSpec((1, TQ, D_MODEL), lambda b, j: (b, j, 0)),
            _resident((1, D_MODEL)),
            _resident((1, D_MODEL)),
            _resident((CONV_WIDTH, D_CONV)),
            pl.BlockSpec((4, TQ, LANES), lambda b, j: (0, j, 0)),
            _resident((D_MODEL, D_IN)),
            _resident((D_CONV, D_MODEL)),
            _resident((D_ATTN, D_MODEL)),
            _resident((D_MODEL, D_MODEL)),
        ],
        out_specs=pl.BlockSpec((1, TQ, D_MODEL), lambda b, j: (b, j, 0)),
        scratch_shapes=[
            pltpu.VMEM((TQ, D_MODEL), _BF),
            pltpu.VMEM((TQ, D_CONV), _BF),
            pltpu.VMEM((TQ, D_ATTN), _BF),
            pltpu.VMEM((TQ, D_ATTN), _BF),
            pltpu.VMEM((TQ, D_ATTN), _F32),
            pltpu.VMEM((2 * N_KV_HEADS, BLOCK + TQ, LANES), _BF),
            pltpu.VMEM((2 * N_KV_HEADS, BLOCK + TQ, 2 * LANES), _BF),
            pltpu.VMEM((SUBLANES, D_CONV), _F32),
            pltpu.VMEM((TQ, D_MODEL), _F32),
            pltpu.VMEM((TQ, D_MODEL), _F32),
        ],
        compiler_params=pltpu.CompilerParams(
            dimension_semantics=("arbitrary", "arbitrary"),
            vmem_limit_bytes=VMEM_LIMIT,
        ),
        name="hybrid_layer",
    )(sinks.astype(_F32), x, g_pre.reshape(1, D_MODEL), g_post.reshape(1, D_MODEL), w_conv,
      tables, w_in.astype(_BF), w_proj_conv.astype(_BF), w_proj_attn.astype(_BF),
      w_out.astype(_BF))


def kernel(x, g_pre, g_post, w_in, w_conv, sinks, w_proj_conv, w_proj_attn, w_out):
    for layer in range(g_pre.shape[0]):
        x = _hybrid_layer(x, g_pre[layer], g_post[layer], w_in[layer], w_conv[layer],
                          sinks[layer], w_proj_conv[layer], w_proj_attn[layer], w_out[layer])
    return x
```

```python
import jax
import jax.numpy as jnp
from jax import lax
from jax.experimental import pallas as pl
from jax.experimental.pallas import tpu as pltpu

D_MODEL = 1024
D_CONV = D_MODEL
CONV_WIDTH = 3
HEAD_DIM = 64
N_HEADS = D_MODEL // HEAD_DIM
N_KV_HEADS = 2
GROUP = N_HEADS // N_KV_HEADS
D_ATTN = N_HEADS * HEAD_DIM
D_KV = N_KV_HEADS * HEAD_DIM
WINDOW = 128
BLOCK = WINDOW
ROPE_THETA = 10000.0
RMS_EPS = 1e-6

OFF_XC = 0
OFF_BG = OFF_XC + D_CONV
OFF_CG = OFF_BG + D_CONV
OFF_ZC = OFF_CG + D_CONV
OFF_Q = OFF_ZC + D_CONV
OFF_K = OFF_Q + D_ATTN
OFF_V = OFF_K + D_KV
OFF_ZA = OFF_V + D_KV
OFF_GA = OFF_ZA + D_ATTN
OFF_GB = OFF_GA + D_MODEL
D_IN = OFF_GB + D_MODEL

LANES = 128
SUBLANES = 8
MXU_N = 256
PAIR = LANES // HEAD_DIM
N_PAIRS = N_HEADS // PAIR
PAIRS_PER_KV = GROUP // PAIR
TQ = 512
EPILOGUE_ROWS = TQ // 2
NEG = -1e30
VMEM_LIMIT = 60 * 1024 * 1024

assert D_KV == LANES and PAIR == 2 and TQ % BLOCK == 0 and D_MODEL % MXU_N == 0

_BF = jnp.bfloat16
_F32 = jnp.float32


def _dot(a, b):
    return jnp.dot(a, b, preferred_element_type=_F32)


def _dot_nt(a, b):
    return lax.dot_general(a, b, (((1,), (1,)), ((), ())), preferred_element_type=_F32)


def _sigmoid(z):
    return 1.0 / (1.0 + jnp.exp(-z))


def _layer_kernel(sink_ref, x_ref, gpre_ref, gpost_ref, wconv_ref, tab_ref,
                  win_ref, wpc_ref, wpa_ref, wout_ref, o_ref,
                  h_scr, ua_scr, ub_scr, q_scr, sza_scr, k_scr, v_scr, ucarry, t_scr, sgb_scr):
    j = pl.program_id(1)

    @pl.when((pl.program_id(0) == 0) & (j == 0))
    def _():
        low = lax.broadcasted_iota(jnp.int32, (BLOCK + TQ, LANES), 1) < HEAD_DIM
        ones_lo = jnp.where(low, 1.0, 0.0).astype(_BF)
        ones_hi = jnp.where(low, 0.0, 1.0).astype(_BF)
        for i in range(2 * N_KV_HEADS):
            v_scr[i, :, LANES:] = ones_hi if i % 2 else ones_lo

    @pl.when(j == 0)
    def _():
        ucarry[...] = jnp.zeros_like(ucarry)
        k_scr[:, 0:BLOCK, :] = jnp.zeros((2 * N_KV_HEADS, BLOCK, LANES), _BF)
        v_scr[:, 0:BLOCK, 0:LANES] = jnp.zeros((2 * N_KV_HEADS, BLOCK, LANES), _BF)

    @pl.when(j != 0)
    def _():
        k_scr[:, 0:BLOCK, :] = k_scr[:, TQ:TQ + BLOCK, :]
        v_scr[:, 0:BLOCK, 0:LANES] = v_scr[:, TQ:TQ + BLOCK, 0:LANES]

    x = x_ref[0]
    ms = jnp.mean(x * x, axis=-1, keepdims=True)
    h_scr[...] = (x * lax.rsqrt(ms + RMS_EPS) * gpre_ref[...]).astype(_BF)

    def proj(off, n):
        lo = off + n * MXU_N
        return _dot(h_scr[...], win_ref[:, lo:lo + MXU_N])

    lane = lax.broadcasted_iota(jnp.int32, (TQ, LANES), 1)
    first_half = (lane & (HEAD_DIM // 2)) == 0
    low_head = lane < HEAD_DIM

    def rope(z, cos, sin):
        rot = jnp.where(first_half, pltpu.roll(z, LANES - HEAD_DIM // 2, 1),
                        pltpu.roll(z, HEAD_DIM // 2, 1))
        return z * cos + rot * sin

    def q_unit(n):
        qf = proj(OFF_Q, n)
        for c in range(MXU_N // LANES):
            qc = rope(qf[:, c * LANES:(c + 1) * LANES], tab_ref[2], tab_ref[3])
            lo = n * MXU_N + c * LANES
            q_scr[:, lo:lo + LANES] = qc.astype(_BF)

    def za_unit(n):
        za = proj(OFF_ZA, n)
        sza_scr[:, n * MXU_N:(n + 1) * MXU_N] = za * _sigmoid(za)

    row = lax.broadcasted_iota(jnp.int32, (TQ, MXU_N), 0)

    def dense_units():
        for n in range(1, D_ATTN // MXU_N):
            q_unit(n)
            yield
            za_unit(n)
            yield
        for n in range(D_CONV // MXU_N):
            cols = slice(n * MXU_N, (n + 1) * MXU_N)
            xc = proj(OFF_XC, n)
            yield
            u = proj(OFF_CG, n) * xc
            prev = ucarry[:, cols]
            p1 = prev[SUBLANES - 1:SUBLANES, :]
            p2 = prev[SUBLANES - 2:SUBLANES - 1, :]
            u1 = jnp.where(row == 0, p1, pltpu.roll(u, 1, 0))
            u2 = jnp.where(row == 0, p2, jnp.where(row == 1, p1, pltpu.roll(u, 2, 0)))
            ucarry[:, cols] = u[TQ - SUBLANES:, :]
            w = wconv_ref[:, cols]
            y = w[0:1, :] * u2 + w[1:2, :] * u1 + w[2:3, :] * u
            yield
            y = proj(OFF_BG, n) * y
            yield
            zc = proj(OFF_ZC, n)
            ua_scr[:, cols] = ((zc * _sigmoid(zc)) * y).astype(_BF)
            yield
        for n in range(D_MODEL // MXU_N):
            cols = slice(n * MXU_N, (n + 1) * MXU_N)
            ya = _dot(ua_scr[...], wpc_ref[:, cols])
            yield
            t_scr[:, cols] = _sigmoid(proj(OFF_GA, n)) * ya
            yield
        for n in range(D_MODEL // MXU_N):
            sgb_scr[:, n * MXU_N:(n + 1) * MXU_N] = _sigmoid(proj(OFF_GB, n))
            yield

    kv = _dot(h_scr[...], win_ref[:, OFF_K:OFF_K + 2 * D_KV])
    k = rope(kv[:, :D_KV], tab_ref[0], tab_ref[1])
    v = kv[:, D_KV:]
    kr = pltpu.roll(k, HEAD_DIM, 1)
    vr = pltpu.roll(v, HEAD_DIM, 1)
    zero = jnp.zeros_like(k)
    k_variants = (jnp.where(low_head, k, zero), jnp.where(low_head, zero, kr),
                  jnp.where(low_head, kr, zero), jnp.where(low_head, zero, k))
    v_variants = (jnp.where(low_head, v, zero), jnp.where(low_head, zero, vr),
                  jnp.where(low_head, vr, zero), jnp.where(low_head, zero, v))
    for i in range(2 * N_KV_HEADS):
        k_scr[i, BLOCK:BLOCK + TQ, :] = k_variants[i].astype(_BF)
        v_scr[i, BLOCK:BLOCK + TQ, 0:LANES] = v_variants[i].astype(_BF)
    q_unit(0)
    za_unit(0)

    qi = lax.broadcasted_iota(jnp.int32, (BLOCK, 2 * BLOCK), 0)
    kj = lax.broadcasted_iota(jnp.int32, (BLOCK, 2 * BLOCK), 1)
    band = (kj > qi) & (kj <= qi + BLOCK)
    low_head_b = lax.broadcasted_iota(jnp.int32, (BLOCK, LANES), 1) < HEAD_DIM

    def softmax_parts(sc, mask, sink):
        sc = jnp.where(mask, sc, NEG)
        m = jnp.maximum(jnp.max(sc, axis=-1, keepdims=True), sink)
        return jnp.exp(sc - m).astype(_BF), jnp.exp(sink - m)

    dense = dense_units()
    next(dense)
    for c in range(N_PAIRS):
        kvh = c // PAIRS_PER_KV
        for b in range(TQ // BLOCK):
            r0 = b * BLOCK
            first_key = jnp.where(j == 0, BLOCK, 0) if b == 0 else 0
            mask = band & (kj >= first_key)
            qp = q_scr[r0:r0 + BLOCK, c * LANES:(c + 1) * LANES]
            ka = k_scr[2 * kvh, r0:r0 + 2 * BLOCK, :]
            kb = k_scr[2 * kvh + 1, r0:r0 + 2 * BLOCK, :]
            va = v_scr[2 * kvh, r0:r0 + 2 * BLOCK, :]
            vb = v_scr[2 * kvh + 1, r0:r0 + 2 * BLOCK, :]
            sa = _dot_nt(qp, ka)
            sb = _dot_nt(qp, kb)
            next(dense, None)
            pa, ea = softmax_parts(sa, mask, sink_ref[PAIR * c])
            pb, eb = softmax_parts(sb, mask, sink_ref[PAIR * c + 1])
            ol = _dot(pa, va) + _dot(pb, vb)
            denom = ol[:, LANES:] + jnp.where(low_head_b, ea, eb)
            gate = sza_scr[r0:r0 + BLOCK, c * LANES:(c + 1) * LANES]
            ub_scr[r0:r0 + BLOCK, c * LANES:(c + 1) * LANES] = (
                gate * (ol[:, :LANES] * (1.0 / denom))).astype(_BF)
    for _ in dense:
        pass

    for r in range(0, TQ, EPILOGUE_ROWS):
        rows = slice(r, r + EPILOGUE_ROWS)
        for n in range(D_MODEL // MXU_N):
            cols = slice(n * MXU_N, (n + 1) * MXU_N)
            yb = _dot(ub_scr[rows, :], wpa_ref[:, cols])
            ua_scr[rows, cols] = (t_scr[rows, cols] + sgb_scr[rows, cols] * yb).astype(_BF)
        y = _dot(ua_scr[rows, :], wout_ref[...])
        ms2 = jnp.mean(y * y, axis=-1, keepdims=True)
        o_ref[0, rows, :] = x_ref[0, rows, :] + y * lax.rsqrt(ms2 + RMS_EPS) * gpost_ref[...]


def _rope_tables(t):
    inv_freq = ROPE_THETA ** (-jnp.arange(0, HEAD_DIM, 2, dtype=_F32) / HEAD_DIM)
    ang = jnp.arange(t).astype(_F32)[:, None] * inv_freq[None, :]
    cos = jnp.cos(ang)
    sin = jnp.sin(ang)
    cos = jnp.tile(jnp.concatenate([cos, cos], axis=-1), (1, PAIR))
    sin = jnp.tile(jnp.concatenate([-sin, sin], axis=-1), (1, PAIR))
    scale = HEAD_DIM ** -0.5
    return jnp.stack([cos, sin, cos * scale, sin * scale])


def _resident(shape):
    return pl.BlockSpec(shape, lambda b, j: (0,) * len(shape), pipeline_mode=pl.Buffered(1))


def _hybrid_layer(x, g_pre, g_post, w_in, w_conv, sinks, w_proj_conv, w_proj_attn, w_out):
    bsz, t, d = x.shape
    assert d == D_MODEL and t % TQ == 0 and w_in.shape == (D_MODEL, D_IN)
    tables = _rope_tables(t)
    return pl.pallas_call(
        _layer_kernel,
        out_shape=jax.ShapeDtypeStruct(x.shape, x.dtype),
        grid=(bsz, t // TQ),
        in_specs=[
            pl.BlockSpec(memory_space=pltpu.SMEM),
            pl.BlockSpec((1, TQ, D_MODEL), lambda b, j: (b, j, 0)),
            _resident((1, D_MODEL)),
            _resident((1, D_MODEL)),
            _resident((CONV_WIDTH, D_CONV)),
            pl.BlockSpec((4, TQ, LANES), lambda b, j: (0, j, 0)),
            _resident((D_MODEL, D_IN)),
            _resident((D_CONV, D_MODEL)),
            _resident((D_ATTN, D_MODEL)),
            _resident((D_MODEL, D_MODEL)),
        ],
        out_specs=pl.BlockSpec((1, TQ, D_MODEL), lambda b, j: (b, j, 0)),
        scratch_shapes=[
            pltpu.VMEM((TQ, D_MODEL), _BF),
            pltpu.VMEM((TQ, D_CONV), _BF),
            pltpu.VMEM((TQ, D_ATTN), _BF),
            pltpu.VMEM((TQ, D_ATTN), _BF),
            pltpu.VMEM((TQ, D_ATTN), _F32),
            pltpu.VMEM((2 * N_KV_HEADS, BLOCK + TQ, LANES), _BF),
            pltpu.VMEM((2 * N_KV_HEADS, BLOCK + TQ, 2 * LANES), _BF),
            pltpu.VMEM((SUBLANES, D_CONV), _F32),
            pltpu.VMEM((TQ, D_MODEL), _F32),
            pltpu.VMEM((TQ, D_MODEL), _F32),
        ],
        compiler_params=pltpu.CompilerParams(
            dimension_semantics=("arbitrary", "arbitrary"),
            vmem_limit_bytes=VMEM_LIMIT,
        ),
        name="hybrid_layer",
    )(sinks.astype(_F32), x, g_pre.reshape(1, D_MODEL), g_post.reshape(1, D_MODEL), w_conv,
      tables, w_in.astype(_BF), w_proj_conv.astype(_BF), w_proj_attn.astype(_BF),
      w_out.astype(_BF))


def kernel(x, g_pre, g_post, w_in, w_conv, sinks, w_proj_conv, w_proj_attn, w_out):
    for layer in range(g_pre.shape[0]):
        x = _hybrid_layer(x, g_pre[layer], g_post[layer], w_in[layer], w_conv[layer],
                          sinks[layer], w_proj_conv[layer], w_proj_attn[layer], w_out[layer])
    return x
```

```python
import jax
import jax.numpy as jnp
from jax import lax
from jax.experimental import pallas as pl
from jax.experimental.pallas import tpu as pltpu

D_MODEL = 1024
D_CONV = D_MODEL
CONV_WIDTH = 3
HEAD_DIM = 64
N_HEADS = D_MODEL // HEAD_DIM
N_KV_HEADS = 2
GROUP = N_HEADS // N_KV_HEADS
D_ATTN = N_HEADS * HEAD_DIM
D_KV = N_KV_HEADS * HEAD_DIM
WINDOW = 128
BLOCK = WINDOW
ROPE_THETA = 10000.0
RMS_EPS = 1e-6

OFF_XC = 0
OFF_BG = OFF_XC + D_CONV
OFF_CG = OFF_BG + D_CONV
OFF_ZC = OFF_CG + D_CONV
OFF_Q = OFF_ZC + D_CONV
OFF_K = OFF_Q + D_ATTN
OFF_V = OFF_K + D_KV
OFF_ZA = OFF_V + D_KV
OFF_GA = OFF_ZA + D_ATTN
OFF_GB = OFF_GA + D_MODEL
D_IN = OFF_GB + D_MODEL

LANES = 128
SUBLANES = 8
MXU_N = 256
PAIR = LANES // HEAD_DIM
N_PAIRS = N_HEADS // PAIR
PAIRS_PER_KV = GROUP // PAIR
PAIR_STACK = 2
assert PAIRS_PER_KV % PAIR_STACK == 0
TQ = 512
EPILOGUE_SPLIT = (0, TQ // 2, TQ)
NEG = -1e30
VMEM_LIMIT = 60 * 1024 * 1024

assert D_KV == LANES and PAIR == 2 and TQ % BLOCK == 0 and D_MODEL % MXU_N == 0

_BF = jnp.bfloat16
_F32 = jnp.float32


def _dot(a, b):
    return jnp.dot(a, b, preferred_element_type=_F32)


def _dot_nt(a, b):
    return lax.dot_general(a, b, (((1,), (1,)), ((), ())), preferred_element_type=_F32)


def _sigmoid(z):
    return 1.0 / (1.0 + jnp.exp(-z))


def _layer_kernel(sink_ref, x_ref, gpre_ref, gpost_ref, wconv_ref, tab_ref,
                  win_ref, wpc_ref, wpa_ref, wout_ref, o_ref,
                  h_scr, ua_scr, ub_scr, q_scr, sza_scr, k_scr, v_scr, ucarry, t_scr, sgb_scr):
    j = pl.program_id(1)

    @pl.when((pl.program_id(0) == 0) & (j == 0))
    def _():
        low = lax.broadcasted_iota(jnp.int32, (BLOCK + TQ, LANES), 1) < HEAD_DIM
        ones_lo = jnp.where(low, 1.0, 0.0).astype(_BF)
        ones_hi = jnp.where(low, 0.0, 1.0).astype(_BF)
        for i in range(2 * N_KV_HEADS):
            v_scr[i, :, LANES:] = ones_hi if i % 2 else ones_lo

    @pl.when(j == 0)
    def _():
        ucarry[...] = jnp.zeros_like(ucarry)
        k_scr[:, 0:BLOCK, :] = jnp.zeros((2 * N_KV_HEADS, BLOCK, LANES), _BF)
        v_scr[:, 0:BLOCK, 0:LANES] = jnp.zeros((2 * N_KV_HEADS, BLOCK, LANES), _BF)

    @pl.when(j != 0)
    def _():
        k_scr[:, 0:BLOCK, :] = k_scr[:, TQ:TQ + BLOCK, :]
        v_scr[:, 0:BLOCK, 0:LANES] = v_scr[:, TQ:TQ + BLOCK, 0:LANES]

    x = x_ref[0]
    ms = jnp.mean(x * x, axis=-1, keepdims=True)
    h_scr[...] = (x * lax.rsqrt(ms + RMS_EPS) * gpre_ref[...]).astype(_BF)

    def proj(off, n):
        lo = off + n * MXU_N
        return _dot(h_scr[...], win_ref[:, lo:lo + MXU_N])

    lane = lax.broadcasted_iota(jnp.int32, (TQ, LANES), 1)
    first_half = (lane & (HEAD_DIM // 2)) == 0
    low_head = lane < HEAD_DIM

    def rope(z, cos, sin):
        rot = jnp.where(first_half, pltpu.roll(z, LANES - HEAD_DIM // 2, 1),
                        pltpu.roll(z, HEAD_DIM // 2, 1))
        return z * cos + rot * sin

    def q_unit(n):
        qf = proj(OFF_Q, n)
        for c in range(MXU_N // LANES):
            qc = rope(qf[:, c * LANES:(c + 1) * LANES], tab_ref[2], tab_ref[3])
            lo = n * MXU_N + c * LANES
            q_scr[:, lo:lo + LANES] = qc.astype(_BF)

    def za_unit(n):
        za = proj(OFF_ZA, n)
        sza_scr[:, n * MXU_N:(n + 1) * MXU_N] = za * _sigmoid(za)

    row = lax.broadcasted_iota(jnp.int32, (TQ, MXU_N), 0)

    def dense_units():
        for n in range(1, D_ATTN // MXU_N):
            q_unit(n)
            yield
            za_unit(n)
            yield
        for n in range(D_CONV // MXU_N):
            cols = slice(n * MXU_N, (n + 1) * MXU_N)
            xc = proj(OFF_XC, n)
            yield
            u = proj(OFF_CG, n) * xc
            prev = ucarry[:, cols]
            p1 = prev[SUBLANES - 1:SUBLANES, :]
            p2 = prev[SUBLANES - 2:SUBLANES - 1, :]
            u1 = jnp.where(row == 0, p1, pltpu.roll(u, 1, 0))
            u2 = jnp.where(row == 0, p2, jnp.where(row == 1, p1, pltpu.roll(u, 2, 0)))
            ucarry[:, cols] = u[TQ - SUBLANES:, :]
            w = wconv_ref[:, cols]
            y = w[0:1, :] * u2 + w[1:2, :] * u1 + w[2:3, :] * u
            yield
            y = proj(OFF_BG, n) * y
            yield
            zc = proj(OFF_ZC, n)
            ua_scr[:, cols] = ((zc * _sigmoid(zc)) * y).astype(_BF)
            yield
        for n in range(D_MODEL // MXU_N):
            cols = slice(n * MXU_N, (n + 1) * MXU_N)
            ya = _dot(ua_scr[...], wpc_ref[:, cols])
            yield
            t_scr[:, cols] = _sigmoid(proj(OFF_GA, n)) * ya
            yield
        for n in range(D_MODEL // MXU_N):
            sgb_scr[:, n * MXU_N:(n + 1) * MXU_N] = _sigmoid(proj(OFF_GB, n))
            yield

    kv = proj(OFF_K, 0)
    k = rope(kv[:, :D_KV], tab_ref[0], tab_ref[1])
    v = kv[:, D_KV:]
    kr = pltpu.roll(k, HEAD_DIM, 1)
    vr = pltpu.roll(v, HEAD_DIM, 1)
    zero = jnp.zeros_like(k)
    k_variants = (jnp.where(low_head, k, zero), jnp.where(low_head, zero, kr),
                  jnp.where(low_head, kr, zero), jnp.where(low_head, zero, k))
    v_variants = (jnp.where(low_head, v, zero), jnp.where(low_head, zero, vr),
                  jnp.where(low_head, vr, zero), jnp.where(low_head, zero, v))
    for i in range(2 * N_KV_HEADS):
        k_scr[i, BLOCK:BLOCK + TQ, :] = k_variants[i].astype(_BF)
        v_scr[i, BLOCK:BLOCK + TQ, 0:LANES] = v_variants[i].astype(_BF)
    q_unit(0)
    za_unit(0)

    qi = lax.broadcasted_iota(jnp.int32, (BLOCK, 2 * BLOCK), 0)
    kj = lax.broadcasted_iota(jnp.int32, (BLOCK, 2 * BLOCK), 1)
    band = (kj > qi) & (kj <= qi + BLOCK)
    low_head_b = lax.broadcasted_iota(jnp.int32, (BLOCK, LANES), 1) < HEAD_DIM

    def softmax_parts(sc, mask, sink):
        sc = jnp.where(mask, sc, NEG)
        m = jnp.maximum(jnp.max(sc, axis=-1, keepdims=True), sink)
        return jnp.exp(sc - m).astype(_BF), jnp.exp(sink - m)

    dense = dense_units()
    next(dense)
    for c0 in range(0, N_PAIRS, PAIR_STACK):
        kvh = c0 // PAIRS_PER_KV
        pairs = range(c0, c0 + PAIR_STACK)
        for b in range(TQ // BLOCK):
            r0 = b * BLOCK
            first_key = jnp.where(j == 0, BLOCK, 0) if b == 0 else 0
            mask = band & (kj >= first_key)
            qs = jnp.concatenate(
                [q_scr[r0:r0 + BLOCK, c * LANES:(c + 1) * LANES] for c in pairs], axis=0)
            ka = k_scr[2 * kvh, r0:r0 + 2 * BLOCK, :]
            kb = k_scr[2 * kvh + 1, r0:r0 + 2 * BLOCK, :]
            va = v_scr[2 * kvh, r0:r0 + 2 * BLOCK, :]
            vb = v_scr[2 * kvh + 1, r0:r0 + 2 * BLOCK, :]
            sa = _dot_nt(qs, ka)
            sb = _dot_nt(qs, kb)
            for _ in range(PAIR_STACK):
                next(dense, None)
            parts_a = [softmax_parts(sa[i * BLOCK:(i + 1) * BLOCK], mask, sink_ref[PAIR * c])
                       for i, c in enumerate(pairs)]
            parts_b = [softmax_parts(sb[i * BLOCK:(i + 1) * BLOCK], mask, sink_ref[PAIR * c + 1])
                       for i, c in enumerate(pairs)]
            pa = jnp.concatenate([p for p, _ in parts_a], axis=0)
            pb = jnp.concatenate([p for p, _ in parts_b], axis=0)
            ols = _dot(pa, va) + _dot(pb, vb)
            for i, c in enumerate(pairs):
                ol = ols[i * BLOCK:(i + 1) * BLOCK]
                denom = ol[:, LANES:] + jnp.where(low_head_b, parts_a[i][1], parts_b[i][1])
                gate = sza_scr[r0:r0 + BLOCK, c * LANES:(c + 1) * LANES]
                ub_scr[r0:r0 + BLOCK, c * LANES:(c + 1) * LANES] = (
                    gate * (ol[:, :LANES] * (1.0 / denom))).astype(_BF)
    for _ in dense:
        pass

    for r0, r1 in zip(EPILOGUE_SPLIT[:-1], EPILOGUE_SPLIT[1:]):
        rows = slice(r0, r1)
        for n in range(D_MODEL // MXU_N):
            cols = slice(n * MXU_N, (n + 1) * MXU_N)
            yb = _dot(ub_scr[rows, :], wpa_ref[:, cols])
            ua_scr[rows, cols] = (t_scr[rows, cols] + sgb_scr[rows, cols] * yb).astype(_BF)
        y = _dot(ua_scr[rows, :], wout_ref[...])
        ms2 = jnp.mean(y * y, axis=-1, keepdims=True)
        o_ref[0, rows, :] = x_ref[0, rows, :] + y * lax.rsqrt(ms2 + RMS_EPS) * gpost_ref[...]


def _rope_tables(t):
    inv_freq = ROPE_THETA ** (-jnp.arange(0, HEAD_DIM, 2, dtype=_F32) / HEAD_DIM)
    ang = jnp.arange(t).astype(_F32)[:, None] * inv_freq[None, :]
    cos = jnp.cos(ang)
    sin = jnp.sin(ang)
    cos = jnp.tile(jnp.concatenate([cos, cos], axis=-1), (1, PAIR))
    sin = jnp.tile(jnp.concatenate([-sin, sin], axis=-1), (1, PAIR))
    scale = HEAD_DIM ** -0.5
    return jnp.stack([cos, sin, cos * scale, sin * scale])


def _resident(shape):
    return pl.BlockSpec(shape, lambda b, j: (0,) * len(shape), pipeline_mode=pl.Buffered(1))


def _hybrid_layer(x, g_pre, g_post, w_in, w_conv, sinks, w_proj_conv, w_proj_attn, w_out):
    bsz, t, d = x.shape
    assert d == D_MODEL and t % TQ == 0 and w_in.shape == (D_MODEL, D_IN)
    tables = _rope_tables(t)
    return pl.pallas_call(
        _layer_kernel,
        out_shape=jax.ShapeDtypeStruct(x.shape, x.dtype),
        grid=(bsz, t // TQ),
        in_specs=[
            pl.BlockSpec(memory_space=pltpu.SMEM),
            pl.BlockSpec((1, TQ, D_MODEL), lambda b, j: (b, j, 0)),
            _resident((1, D_MODEL)),
            _resident((1, D_MODEL)),
            _resident((CONV_WIDTH, D_CONV)),
            pl.BlockSpec((4, TQ, LANES), lambda b, j: (0, j, 0)),
            _resident((D_MODEL, D_IN)),
            _resident((D_CONV, D_MODEL)),
            _resident((D_ATTN, D_MODEL)),
            _resident((D_MODEL, D_MODEL)),
        ],
        out_specs=pl.BlockSpec((1, TQ, D_MODEL), lambda b, j: (b, j, 0)),
        scratch_shapes=[
            pltpu.VMEM((TQ, D_MODEL), _BF),
            pltpu.VMEM((TQ, D_CONV), _BF),
            pltpu.VMEM((TQ, D_ATTN), _BF),
            pltpu.VMEM((TQ, D_ATTN), _BF),
            pltpu.VMEM((TQ, D_ATTN), _F32),
            pltpu.VMEM((2 * N_KV_HEADS, BLOCK + TQ, LANES), _BF),
            pltpu.VMEM((2 * N_KV_HEADS, BLOCK + TQ, 2 * LANES), _BF),
            pltpu.VMEM((SUBLANES, D_CONV), _F32),
            pltpu.VMEM((TQ, D_MODEL), _F32),
            pltpu.VMEM((TQ, D_MODEL), _F32),
        ],
        compiler_params=pltpu.CompilerParams(
            dimension_semantics=("arbitrary", "arbitrary"),
            vmem_limit_bytes=VMEM_LIMIT,
        ),
        name="hybrid_layer",
    )(sinks.astype(_F32), x, g_pre.reshape(1, D_MODEL), g_post.reshape(1, D_MODEL), w_conv,
      tables, w_in.astype(_BF), w_proj_conv.astype(_BF), w_proj_attn.astype(_BF),
      w_out.astype(_BF))


def kernel(x, g_pre, g_post, w_in, w_conv, sinks, w_proj_conv, w_proj_attn, w_out):
    for layer in range(g_pre.shape[0]):
        x = _hybrid_layer(x, g_pre[layer], g_post[layer], w_in[layer], w_conv[layer],
                          sinks[layer], w_proj_conv[layer], w_proj_attn[layer], w_out[layer])
    return x
```

```python
import jax
import jax.numpy as jnp
from jax import lax
from jax.experimental import pallas as pl
from jax.experimental.pallas import tpu as pltpu

D_MODEL = 1024
D_CONV = D_MODEL
CONV_WIDTH = 3
HEAD_DIM = 64
N_HEADS = D_MODEL // HEAD_DIM
N_KV_HEADS = 2
GROUP = N_HEADS // N_KV_HEADS
D_ATTN = N_HEADS * HEAD_DIM
D_KV = N_KV_HEADS * HEAD_DIM
WINDOW = 128
BLOCK = WINDOW
ROPE_THETA = 10000.0
RMS_EPS = 1e-6

OFF_XC = 0
OFF_BG = OFF_XC + D_CONV
OFF_CG = OFF_BG + D_CONV
OFF_ZC = OFF_CG + D_CONV
OFF_Q = OFF_ZC + D_CONV
OFF_K = OFF_Q + D_ATTN
OFF_V = OFF_K + D_KV
OFF_ZA = OFF_V + D_KV
OFF_GA = OFF_ZA + D_ATTN
OFF_GB = OFF_GA + D_MODEL
D_IN = OFF_GB + D_MODEL

LANES = 128
SUBLANES = 8
MXU_N = 256
PAIR = LANES // HEAD_DIM
N_PAIRS = N_HEADS // PAIR
PAIRS_PER_KV = GROUP // PAIR
PAIR_STACK = 2
assert PAIRS_PER_KV % PAIR_STACK == 0
N_DENSE_UNITS = (2 * (D_ATTN // MXU_N - 1) + 4 * (D_CONV // MXU_N) + 2 * (D_MODEL // MXU_N)
                 + D_MODEL // MXU_N)
TQ = 1024
DOT_ROWS = 512
EPILOGUE_ROWS = 256
EPILOGUE_SPLIT = tuple(range(0, TQ + 1, EPILOGUE_ROWS))
NEG = -1e30
VMEM_LIMIT = 62 * 1024 * 1024

assert D_KV == LANES and PAIR == 2 and TQ % BLOCK == 0 and D_MODEL % MXU_N == 0

_BF = jnp.bfloat16
_F32 = jnp.float32


def _dot(a, b):
    m = a.shape[0]
    if m <= DOT_ROWS:
        return jnp.dot(a, b, preferred_element_type=_F32)
    return jnp.concatenate(
        [jnp.dot(a[r:r + DOT_ROWS], b, preferred_element_type=_F32) for r in range(0, m, DOT_ROWS)],
        axis=0)


def _dot_nt(a, b):
    return lax.dot_general(a, b, (((1,), (1,)), ((), ())), preferred_element_type=_F32)


def _sigmoid(z):
    return 1.0 / (1.0 + jnp.exp(-z))


def _layer_kernel(sink_ref, x_ref, gpre_ref, gpost_ref, wconv_ref, tab_ref,
                  win_ref, wpc_ref, wpa_ref, wout_ref, o_ref,
                  h_scr, ua_scr, ub_scr, q_scr, sza_scr, k_scr, v_scr, ucarry, t_scr, sgb_scr):
    j = pl.program_id(1)

    @pl.when((pl.program_id(0) == 0) & (j == 0))
    def _():
        low = lax.broadcasted_iota(jnp.int32, (BLOCK + TQ, LANES), 1) < HEAD_DIM
        ones_lo = jnp.where(low, 1.0, 0.0).astype(_BF)
        ones_hi = jnp.where(low, 0.0, 1.0).astype(_BF)
        for i in range(2 * N_KV_HEADS):
            v_scr[i, :, LANES:] = ones_hi if i % 2 else ones_lo

    @pl.when(j == 0)
    def _():
        ucarry[...] = jnp.zeros_like(ucarry)
        k_scr[:, 0:BLOCK, :] = jnp.zeros((2 * N_KV_HEADS, BLOCK, LANES), _BF)
        v_scr[:, 0:BLOCK, 0:LANES] = jnp.zeros((2 * N_KV_HEADS, BLOCK, LANES), _BF)

    @pl.when(j != 0)
    def _():
        k_scr[:, 0:BLOCK, :] = k_scr[:, TQ:TQ + BLOCK, :]
        v_scr[:, 0:BLOCK, 0:LANES] = v_scr[:, TQ:TQ + BLOCK, 0:LANES]

    x = x_ref[0]
    ms = jnp.mean(x * x, axis=-1, keepdims=True)
    h_scr[...] = (x * lax.rsqrt(ms + RMS_EPS) * gpre_ref[...]).astype(_BF)

    def proj(off, n):
        lo = off + n * MXU_N
        return _dot(h_scr[...], win_ref[:, lo:lo + MXU_N])

    lane = lax.broadcasted_iota(jnp.int32, (TQ, LANES), 1)
    first_half = (lane & (HEAD_DIM // 2)) == 0
    low_head = lane < HEAD_DIM

    def rope(z, cos, sin):
        rot = jnp.where(first_half, pltpu.roll(z, LANES - HEAD_DIM // 2, 1),
                        pltpu.roll(z, HEAD_DIM // 2, 1))
        return z * cos + rot * sin

    def q_unit(n):
        qf = proj(OFF_Q, n) * (HEAD_DIM ** -0.5)
        for c in range(MXU_N // LANES):
            qc = rope(qf[:, c * LANES:(c + 1) * LANES], tab_ref[0], tab_ref[1])
            lo = n * MXU_N + c * LANES
            q_scr[:, lo:lo + LANES] = qc.astype(_BF)

    def za_unit(n):
        za = proj(OFF_ZA, n)
        sza_scr[:, n * MXU_N:(n + 1) * MXU_N] = (za * _sigmoid(za)).astype(_BF)

    row = lax.broadcasted_iota(jnp.int32, (TQ, MXU_N), 0)

    def dense_units():
        for n in range(1, D_ATTN // MXU_N):
            q_unit(n)
            yield
            za_unit(n)
            yield
        for n in range(D_CONV // MXU_N):
            cols = slice(n * MXU_N, (n + 1) * MXU_N)
            xc = proj(OFF_XC, n)
            yield
            u = proj(OFF_CG, n) * xc
            prev = ucarry[:, cols]
            p1 = prev[SUBLANES - 1:SUBLANES, :]
            p2 = prev[SUBLANES - 2:SUBLANES - 1, :]
            u1 = jnp.where(row == 0, p1, pltpu.roll(u, 1, 0))
            u2 = jnp.where(row == 0, p2, jnp.where(row == 1, p1, pltpu.roll(u, 2, 0)))
            ucarry[:, cols] = u[TQ - SUBLANES:, :]
            w = wconv_ref[:, cols]
            y = w[0:1, :] * u2 + w[1:2, :] * u1 + w[2:3, :] * u
            yield
            y = proj(OFF_BG, n) * y
            yield
            zc = proj(OFF_ZC, n)
            ua_scr[:, cols] = ((zc * _sigmoid(zc)) * y).astype(_BF)
            yield
        for n in range(D_MODEL // MXU_N):
            cols = slice(n * MXU_N, (n + 1) * MXU_N)
            ya = _dot(ua_scr[...], wpc_ref[:, cols])
            yield
            t_scr[:, cols] = _sigmoid(proj(OFF_GA, n)) * ya
            yield
        for n in range(D_MODEL // MXU_N):
            sgb_scr[:, n * MXU_N:(n + 1) * MXU_N] = _sigmoid(proj(OFF_GB, n)).astype(_BF)
            yield

    kv = proj(OFF_K, 0)
    k = rope(kv[:, :D_KV], tab_ref[0], tab_ref[1])
    v = kv[:, D_KV:]
    kr = pltpu.roll(k, HEAD_DIM, 1)
    vr = pltpu.roll(v, HEAD_DIM, 1)
    zero = jnp.zeros_like(k)
    k_variants = (jnp.where(low_head, k, zero), jnp.where(low_head, zero, kr),
                  jnp.where(low_head, kr, zero), jnp.where(low_head, zero, k))
    v_variants = (jnp.where(low_head, v, zero), jnp.where(low_head, zero, vr),
                  jnp.where(low_head, vr, zero), jnp.where(low_head, zero, v))
    for i in range(2 * N_KV_HEADS):
        k_scr[i, BLOCK:BLOCK + TQ, :] = k_variants[i].astype(_BF)
        v_scr[i, BLOCK:BLOCK + TQ, 0:LANES] = v_variants[i].astype(_BF)
    q_unit(0)
    za_unit(0)

    qi = lax.broadcasted_iota(jnp.int32, (BLOCK, 2 * BLOCK), 0)
    kj = lax.broadcasted_iota(jnp.int32, (BLOCK, 2 * BLOCK), 1)
    band = (kj > qi) & (kj <= qi + BLOCK)
    low_head_b = lax.broadcasted_iota(jnp.int32, (BLOCK, LANES), 1) < HEAD_DIM

    def softmax_parts(sc, mask, sink):
        sc = jnp.where(mask, sc, NEG)
        m = jnp.maximum(jnp.max(sc, axis=-1, keepdims=True), sink)
        return jnp.exp(sc - m).astype(_BF), jnp.exp(sink - m)

    dense = dense_units()
    next(dense)
    n_iters = (N_PAIRS // PAIR_STACK) * (TQ // BLOCK)
    n_left = N_DENSE_UNITS - 1
    it = 0
    for c0 in range(0, N_PAIRS, PAIR_STACK):
        kvh = c0 // PAIRS_PER_KV
        pairs = range(c0, c0 + PAIR_STACK)
        for b in range(TQ // BLOCK):
            r0 = b * BLOCK
            first_key = jnp.where(j == 0, BLOCK, 0) if b == 0 else 0
            mask = band & (kj >= first_key)
            qs = jnp.concatenate(
                [q_scr[r0:r0 + BLOCK, c * LANES:(c + 1) * LANES] for c in pairs], axis=0)
            ka = k_scr[2 * kvh, r0:r0 + 2 * BLOCK, :]
            kb = k_scr[2 * kvh + 1, r0:r0 + 2 * BLOCK, :]
            va = v_scr[2 * kvh, r0:r0 + 2 * BLOCK, :]
            vb = v_scr[2 * kvh + 1, r0:r0 + 2 * BLOCK, :]
            sa = _dot_nt(qs, ka)
            sb = _dot_nt(qs, kb)
            for _ in range(n_left * (it + 1) // n_iters - n_left * it // n_iters):
                next(dense)
            it += 1
            parts_a = [softmax_parts(sa[i * BLOCK:(i + 1) * BLOCK], mask, sink_ref[PAIR * c])
                       for i, c in enumerate(pairs)]
            parts_b = [softmax_parts(sb[i * BLOCK:(i + 1) * BLOCK], mask, sink_ref[PAIR * c + 1])
                       for i, c in enumerate(pairs)]
            pa = jnp.concatenate([p for p, _ in parts_a], axis=0)
            pb = jnp.concatenate([p for p, _ in parts_b], axis=0)
            ols = _dot(pa, va) + _dot(pb, vb)
            for i, c in enumerate(pairs):
                ol = ols[i * BLOCK:(i + 1) * BLOCK]
                denom = ol[:, LANES:] + jnp.where(low_head_b, parts_a[i][1], parts_b[i][1])
                gate = sza_scr[r0:r0 + BLOCK, c * LANES:(c + 1) * LANES]
                ub_scr[r0:r0 + BLOCK, c * LANES:(c + 1) * LANES] = (
                    gate * (ol[:, :LANES] * (1.0 / denom))).astype(_BF)
    for _ in dense:
        pass

    for r0, r1 in zip(EPILOGUE_SPLIT[:-1], EPILOGUE_SPLIT[1:]):
        rows = slice(r0, r1)
        for n in range(D_MODEL // MXU_N):
            cols = slice(n * MXU_N, (n + 1) * MXU_N)
            yb = _dot(ub_scr[rows, :], wpa_ref[:, cols])
            ua_scr[rows, cols] = (t_scr[rows, cols] + sgb_scr[rows, cols] * yb).astype(_BF)
        y = _dot(ua_scr[rows, :], wout_ref[...])
        ms2 = jnp.mean(y * y, axis=-1, keepdims=True)
        o_ref[0, rows, :] = x_ref[0, rows, :] + y * lax.rsqrt(ms2 + RMS_EPS) * gpost_ref[...]


def _rope_tables(t):
    inv_freq = ROPE_THETA ** (-jnp.arange(0, HEAD_DIM, 2, dtype=_F32) / HEAD_DIM)
    inv_freq = jnp.tile(inv_freq, 2 * PAIR)
    half = jnp.ones((HEAD_DIM // 2,), _F32)
    sign = jnp.tile(jnp.concatenate([-half, half]), PAIR)
    ang = jnp.arange(t).astype(_F32)[:, None] * inv_freq[None, :]
    return jnp.stack([jnp.cos(ang), jnp.sin(ang) * sign])


def _resident(shape):
    return pl.BlockSpec(shape, lambda b, j: (0,) * len(shape), pipeline_mode=pl.Buffered(1))


def _hybrid_layer(x, g_pre, g_post, w_in, w_conv, sinks, w_proj_conv, w_proj_attn, w_out):
    bsz, t, d = x.shape
    assert d == D_MODEL and t % TQ == 0 and w_in.shape == (D_MODEL, D_IN)
    tables = _rope_tables(t)
    return pl.pallas_call(
        _layer_kernel,
        out_shape=jax.ShapeDtypeStruct(x.shape, x.dtype),
        grid=(bsz, t // TQ),
        in_specs=[
            pl.BlockSpec(memory_space=pltpu.SMEM),
            pl.BlockSpec((1, TQ, D_MODEL), lambda b, j: (b, j, 0)),
            _resident((1, D_MODEL)),
            _resident((1, D_MODEL)),
            _resident((CONV_WIDTH, D_CONV)),
            pl.BlockSpec((2, TQ, LANES), lambda b, j: (0, j, 0)),
            _resident((D_MODEL, D_IN)),
            _resident((D_CONV, D_MODEL)),
            _resident((D_ATTN, D_MODEL)),
            _resident((D_MODEL, D_MODEL)),
        ],
        out_specs=pl.BlockSpec((1, TQ, D_MODEL), lambda b, j: (b, j, 0)),
        scratch_shapes=[
            pltpu.VMEM((TQ, D_MODEL), _BF),
            pltpu.VMEM((TQ, D_CONV), _BF),
            pltpu.VMEM((TQ, D_ATTN), _BF),
            pltpu.VMEM((TQ, D_ATTN), _BF),
            pltpu.VMEM((TQ, D_ATTN), _BF),
            pltpu.VMEM((2 * N_KV_HEADS, BLOCK + TQ, LANES), _BF),
            pltpu.VMEM((2 * N_KV_HEADS, BLOCK + TQ, 2 * LANES), _BF),
            pltpu.VMEM((SUBLANES, D_CONV), _F32),
            pltpu.VMEM((TQ, D_MODEL), _F32),
            pltpu.VMEM((TQ, D_MODEL), _BF),
        ],
        compiler_params=pltpu.CompilerParams(
            dimension_semantics=("arbitrary", "arbitrary"),
            vmem_limit_bytes=VMEM_LIMIT,
        ),
        name="hybrid_layer",
    )(sinks.astype(_F32), x, g_pre.reshape(1, D_MODEL), g_post.reshape(1, D_MODEL), w_conv,
      tables, w_in.astype(_BF), w_proj_conv.astype(_BF), w_proj_attn.astype(_BF),
      w_out.astype(_BF))


def kernel(x, g_pre, g_post, w_in, w_conv, sinks, w_proj_conv, w_proj_attn, w_out):
    for layer in range(g_pre.shape[0]):
        x = _hybrid_layer(x, g_pre[layer], g_post[layer], w_in[layer], w_conv[layer],
                          sinks[layer], w_proj_conv[layer], w_proj_attn[layer], w_out[layer])
    return x
```

```python
import jax
import jax.numpy as jnp
from jax import lax
from jax.experimental import pallas as pl
from jax.experimental.pallas import tpu as pltpu

D_MODEL = 1024
D_CONV = D_MODEL
CONV_WIDTH = 3
HEAD_DIM = 64
N_HEADS = D_MODEL // HEAD_DIM
N_KV_HEADS = 2
GROUP = N_HEADS // N_KV_HEADS
D_ATTN = N_HEADS * HEAD_DIM
D_KV = N_KV_HEADS * HEAD_DIM
WINDOW = 128
BLOCK = WINDOW
ROPE_THETA = 10000.0
RMS_EPS = 1e-6

OFF_XC = 0
OFF_BG = OFF_XC + D_CONV
OFF_CG = OFF_BG + D_CONV
OFF_ZC = OFF_CG + D_CONV
OFF_Q = OFF_ZC + D_CONV
OFF_K = OFF_Q + D_ATTN
OFF_V = OFF_K + D_KV
OFF_ZA = OFF_V + D_KV
OFF_GA = OFF_ZA + D_ATTN
OFF_GB = OFF_GA + D_MODEL
D_IN = OFF_GB + D_MODEL

LANES = 128
SUBLANES = 8
MXU_N = 256
PAIR = LANES // HEAD_DIM
N_PAIRS = N_HEADS // PAIR
PAIRS_PER_KV = GROUP // PAIR
PAIR_STACK = 2
N_DENSE_UNITS = (2 * (D_ATTN // MXU_N - 1) + 4 * (D_CONV // MXU_N) + 2 * (D_MODEL // MXU_N)
                 + D_MODEL // MXU_N)
TQ = 1024
DOT_ROWS = 512
EPILOGUE_ROWS = 256
EPILOGUE_SPLIT = tuple(range(0, TQ + 1, EPILOGUE_ROWS))
NEG = -1e30
VMEM_LIMIT = 62 * 1024 * 1024

assert D_KV == LANES and PAIR == 2 and PAIRS_PER_KV % PAIR_STACK == 0 and D_MODEL % MXU_N == 0
assert TQ % BLOCK == 0 and TQ % DOT_ROWS == 0 and TQ % EPILOGUE_ROWS == 0

_BF = jnp.bfloat16
_F32 = jnp.float32


def _dot(a, b):
    m = a.shape[0]
    if m <= DOT_ROWS:
        return jnp.dot(a, b, preferred_element_type=_F32)
    return jnp.concatenate(
        [jnp.dot(a[r:r + DOT_ROWS], b, preferred_element_type=_F32) for r in range(0, m, DOT_ROWS)],
        axis=0)


def _dot_nt(a, b):
    return lax.dot_general(a, b, (((1,), (1,)), ((), ())), preferred_element_type=_F32)


def _sigmoid(z):
    return 1.0 / (1.0 + jnp.exp(-z))


def _layer_kernel(sink_ref, x_ref, gpre_ref, gpost_ref, wconv_ref, tab_ref,
                  win_ref, wpc_ref, wpa_ref, wout_ref, o_ref,
                  h_scr, ua_scr, ub_scr, q_scr, sza_scr, k_scr, v_scr, ucarry, t_scr, sgb_scr):
    j = pl.program_id(1)

    @pl.when((pl.program_id(0) == 0) & (j == 0))
    def _():
        low = lax.broadcasted_iota(jnp.int32, (BLOCK + TQ, LANES), 1) < HEAD_DIM
        ones_lo = jnp.where(low, 1.0, 0.0).astype(_BF)
        ones_hi = jnp.where(low, 0.0, 1.0).astype(_BF)
        for i in range(2 * N_KV_HEADS):
            v_scr[i, :, LANES:] = ones_hi if i % 2 else ones_lo

    @pl.when(j == 0)
    def _():
        ucarry[...] = jnp.zeros_like(ucarry)
        k_scr[:, 0:BLOCK, :] = jnp.zeros((2 * N_KV_HEADS, BLOCK, LANES), _BF)
        v_scr[:, 0:BLOCK, 0:LANES] = jnp.zeros((2 * N_KV_HEADS, BLOCK, LANES), _BF)

    @pl.when(j != 0)
    def _():
        k_scr[:, 0:BLOCK, :] = k_scr[:, TQ:TQ + BLOCK, :]
        v_scr[:, 0:BLOCK, 0:LANES] = v_scr[:, TQ:TQ + BLOCK, 0:LANES]

    x = x_ref[0]
    ms = jnp.mean(x * x, axis=-1, keepdims=True)
    h_scr[...] = (x * lax.rsqrt(ms + RMS_EPS) * gpre_ref[...]).astype(_BF)

    def proj(off, n):
        lo = off + n * MXU_N
        return _dot(h_scr[...], win_ref[:, lo:lo + MXU_N])

    lane = lax.broadcasted_iota(jnp.int32, (TQ, LANES), 1)
    first_half = (lane & (HEAD_DIM // 2)) == 0
    low_head = lane < HEAD_DIM

    def rope(z, cos, sin):
        rot = jnp.where(first_half, pltpu.roll(z, LANES - HEAD_DIM // 2, 1),
                        pltpu.roll(z, HEAD_DIM // 2, 1))
        return z * cos + rot * sin

    def q_unit(n):
        qf = proj(OFF_Q, n) * (HEAD_DIM ** -0.5)
        for c in range(MXU_N // LANES):
            qc = rope(qf[:, c * LANES:(c + 1) * LANES], tab_ref[0], tab_ref[1])
            lo = n * MXU_N + c * LANES
            q_scr[:, lo:lo + LANES] = qc.astype(_BF)

    def za_unit(n):
        za = proj(OFF_ZA, n)
        sza_scr[:, n * MXU_N:(n + 1) * MXU_N] = (za * _sigmoid(za)).astype(_BF)

    row = lax.broadcasted_iota(jnp.int32, (TQ, MXU_N), 0)

    def dense_units():
        for n in range(1, D_ATTN // MXU_N):
            q_unit(n)
            yield
            za_unit(n)
            yield
        for n in range(D_CONV // MXU_N):
            cols = slice(n * MXU_N, (n + 1) * MXU_N)
            xc = proj(OFF_XC, n)
            yield
            u = proj(OFF_CG, n) * xc
            prev = ucarry[:, cols]
            p1 = prev[SUBLANES - 1:SUBLANES, :]
            p2 = prev[SUBLANES - 2:SUBLANES - 1, :]
            u1 = jnp.where(row == 0, p1, pltpu.roll(u, 1, 0))
            u2 = jnp.where(row == 0, p2, jnp.where(row == 1, p1, pltpu.roll(u, 2, 0)))
            ucarry[:, cols] = u[TQ - SUBLANES:, :]
            w = wconv_ref[:, cols]
            y = w[0:1, :] * u2 + w[1:2, :] * u1 + w[2:3, :] * u
            yield
            y = proj(OFF_BG, n) * y
            yield
            zc = proj(OFF_ZC, n)
            ua_scr[:, cols] = ((zc * _sigmoid(zc)) * y).astype(_BF)
            yield
        for n in range(D_MODEL // MXU_N):
            cols = slice(n * MXU_N, (n + 1) * MXU_N)
            ya = _dot(ua_scr[...], wpc_ref[:, cols])
            yield
            t_scr[:, cols] = _sigmoid(proj(OFF_GA, n)) * ya
            yield
        for n in range(D_MODEL // MXU_N):
            sgb_scr[:, n * MXU_N:(n + 1) * MXU_N] = _sigmoid(proj(OFF_GB, n)).astype(_BF)
            yield

    kv = proj(OFF_K, 0)
    k = rope(kv[:, :D_KV], tab_ref[0], tab_ref[1])
    v = kv[:, D_KV:]
    kr = pltpu.roll(k, HEAD_DIM, 1)
    vr = pltpu.roll(v, HEAD_DIM, 1)
    zero = jnp.zeros_like(k)
    k_variants = (jnp.where(low_head, k, zero), jnp.where(low_head, zero, kr),
                  jnp.where(low_head, kr, zero), jnp.where(low_head, zero, k))
    v_variants = (jnp.where(low_head, v, zero), jnp.where(low_head, zero, vr),
                  jnp.where(low_head, vr, zero), jnp.where(low_head, zero, v))
    for i in range(2 * N_KV_HEADS):
        k_scr[i, BLOCK:BLOCK + TQ, :] = k_variants[i].astype(_BF)
        v_scr[i, BLOCK:BLOCK + TQ, 0:LANES] = v_variants[i].astype(_BF)
    q_unit(0)
    za_unit(0)

    qi = lax.broadcasted_iota(jnp.int32, (BLOCK, 2 * BLOCK), 0)
    kj = lax.broadcasted_iota(jnp.int32, (BLOCK, 2 * BLOCK), 1)
    band = (kj > qi) & (kj <= qi + BLOCK)
    low_head_b = lax.broadcasted_iota(jnp.int32, (BLOCK, LANES), 1) < HEAD_DIM

    def softmax_parts(sc, mask, sink):
        sc = jnp.where(mask, sc, NEG)
        m = jnp.maximum(jnp.max(sc, axis=-1, keepdims=True), sink)
        return jnp.exp(sc - m).astype(_BF), jnp.exp(sink - m)

    dense = dense_units()
    next(dense)
    n_iters = (N_PAIRS // PAIR_STACK) * (TQ // BLOCK)
    n_left = N_DENSE_UNITS - 1
    it = 0
    for c0 in range(0, N_PAIRS, PAIR_STACK):
        kvh = c0 // PAIRS_PER_KV
        pairs = range(c0, c0 + PAIR_STACK)
        for b in range(TQ // BLOCK):
            r0 = b * BLOCK
            first_key = jnp.where(j == 0, BLOCK, 0) if b == 0 else 0
            mask = band & (kj >= first_key)
            qs = jnp.concatenate(
                [q_scr[r0:r0 + BLOCK, c * LANES:(c + 1) * LANES] for c in pairs], axis=0)
            ka = k_scr[2 * kvh, r0:r0 + 2 * BLOCK, :]
            kb = k_scr[2 * kvh + 1, r0:r0 + 2 * BLOCK, :]
            va = v_scr[2 * kvh, r0:r0 + 2 * BLOCK, :]
            vb = v_scr[2 * kvh + 1, r0:r0 + 2 * BLOCK, :]
            sa = _dot_nt(qs, ka)
            sb = _dot_nt(qs, kb)
            for _ in range(n_left * (it + 1) // n_iters - n_left * it // n_iters):
                next(dense)
            it += 1
            parts_a = [softmax_parts(sa[i * BLOCK:(i + 1) * BLOCK], mask, sink_ref[PAIR * c])
                       for i, c in enumerate(pairs)]
            parts_b = [softmax_parts(sb[i * BLOCK:(i + 1) * BLOCK], mask, sink_ref[PAIR * c + 1])
                       for i, c in enumerate(pairs)]
            pa = jnp.concatenate([p for p, _ in parts_a], axis=0)
            pb = jnp.concatenate([p for p, _ in parts_b], axis=0)
            ols = _dot(pa, va) + _dot(pb, vb)
            for i, c in enumerate(pairs):
                ol = ols[i * BLOCK:(i + 1) * BLOCK]
                denom = ol[:, LANES:] + jnp.where(low_head_b, parts_a[i][1], parts_b[i][1])
                gate = sza_scr[r0:r0 + BLOCK, c * LANES:(c + 1) * LANES]
                ub_scr[r0:r0 + BLOCK, c * LANES:(c + 1) * LANES] = (
                    gate * (ol[:, :LANES] * (1.0 / denom))).astype(_BF)
    for _ in dense:
        pass

    for r0, r1 in zip(EPILOGUE_SPLIT[:-1], EPILOGUE_SPLIT[1:]):
        rows = slice(r0, r1)
        for n in range(D_MODEL // MXU_N):
            cols = slice(n * MXU_N, (n + 1) * MXU_N)
            yb = _dot(ub_scr[rows, :], wpa_ref[:, cols])
            ua_scr[rows, cols] = (t_scr[rows, cols] + sgb_scr[rows, cols] * yb).astype(_BF)
        y = _dot(ua_scr[rows, :], wout_ref[...])
        ms2 = jnp.mean(y * y, axis=-1, keepdims=True)
        o_ref[0, rows, :] = x_ref[0, rows, :] + y * lax.rsqrt(ms2 + RMS_EPS) * gpost_ref[...]


def _rope_tables(t):
    inv_freq = ROPE_THETA ** (-jnp.arange(0, HEAD_DIM, 2, dtype=_F32) / HEAD_DIM)
    inv_freq = jnp.tile(inv_freq, 2 * PAIR)
    half = jnp.ones((HEAD_DIM // 2,), _F32)
    sign = jnp.tile(jnp.concatenate([-half, half]), PAIR)
    ang = jnp.arange(t).astype(_F32)[:, None] * inv_freq[None, :]
    return jnp.stack([jnp.cos(ang), jnp.sin(ang) * sign])


def _resident(shape):
    return pl.BlockSpec(shape, lambda b, j: (0,) * len(shape), pipeline_mode=pl.Buffered(1))


def _hybrid_layer(x, g_pre, g_post, w_in, w_conv, sinks, w_proj_conv, w_proj_attn, w_out):
    bsz, t, d = x.shape
    assert d == D_MODEL and t % TQ == 0 and w_in.shape == (D_MODEL, D_IN)
    tables = _rope_tables(t)
    return pl.pallas_call(
        _layer_kernel,
        out_shape=jax.ShapeDtypeStruct(x.shape, x.dtype),
        grid=(bsz, t // TQ),
        in_specs=[
            pl.BlockSpec(memory_space=pltpu.SMEM),
            pl.BlockSpec((1, TQ, D_MODEL), lambda b, j: (b, j, 0)),
            _resident((1, D_MODEL)),
            _resident((1, D_MODEL)),
            _resident((CONV_WIDTH, D_CONV)),
            pl.BlockSpec((2, TQ, LANES), lambda b, j: (0, j, 0)),
            _resident((D_MODEL, D_IN)),
            _resident((D_CONV, D_MODEL)),
            _resident((D_ATTN, D_MODEL)),
            _resident((D_MODEL, D_MODEL)),
        ],
        out_specs=pl.BlockSpec((1, TQ, D_MODEL), lambda b, j: (b, j, 0)),
        scratch_shapes=[
            pltpu.VMEM((TQ, D_MODEL), _BF),
            pltpu.VMEM((TQ, D_CONV), _BF),
            pltpu.VMEM((TQ, D_ATTN), _BF),
            pltpu.VMEM((TQ, D_ATTN), _BF),
            pltpu.VMEM((TQ, D_ATTN), _BF),
            pltpu.VMEM((2 * N_KV_HEADS, BLOCK + TQ, LANES), _BF),
            pltpu.VMEM((2 * N_KV_HEADS, BLOCK + TQ, 2 * LANES), _BF),
            pltpu.VMEM((SUBLANES, D_CONV), _F32),
            pltpu.VMEM((TQ, D_MODEL), _F32),
            pltpu.VMEM((TQ, D_MODEL), _BF),
        ],
        compiler_params=pltpu.CompilerParams(
            dimension_semantics=("arbitrary", "arbitrary"),
            vmem_limit_bytes=VMEM_LIMIT,
        ),
        name="hybrid_layer",
    )(sinks.astype(_F32), x, g_pre.reshape(1, D_MODEL), g_post.reshape(1, D_MODEL), w_conv,
      tables, w_in.astype(_BF), w_proj_conv.astype(_BF), w_proj_attn.astype(_BF),
      w_out.astype(_BF))


def kernel(x, g_pre, g_post, w_in, w_conv, sinks, w_proj_conv, w_proj_attn, w_out):
    for layer in range(g_pre.shape[0]):
        x = _hybrid_layer(x, g_pre[layer], g_post[layer], w_in[layer], w_conv[layer],
                          sinks[layer], w_proj_conv[layer], w_proj_attn[layer], w_out[layer])
    return x
```

```python
import jax
import jax.numpy as jnp
from jax import lax
from jax.experimental import pallas as pl
from jax.experimental.pallas import tpu as pltpu

D_MODEL = 1024
D_CONV = D_MODEL
CONV_WIDTH = 3
HEAD_DIM = 64
N_HEADS = D_MODEL // HEAD_DIM
N_KV_HEADS = 2
GROUP = N_HEADS // N_KV_HEADS
D_ATTN = N_HEADS * HEAD_DIM
D_KV = N_KV_HEADS * HEAD_DIM
WINDOW = 128
BLOCK = WINDOW
ROPE_THETA = 10000.0
RMS_EPS = 1e-6

OFF_XC = 0
OFF_BG = OFF_XC + D_CONV
OFF_CG = OFF_BG + D_CONV
OFF_ZC = OFF_CG + D_CONV
OFF_Q = OFF_ZC + D_CONV
OFF_K = OFF_Q + D_ATTN
OFF_V = OFF_K + D_KV
OFF_ZA = OFF_V + D_KV
OFF_GA = OFF_ZA + D_ATTN
OFF_GB = OFF_GA + D_MODEL
D_IN = OFF_GB + D_MODEL

LANES = 128
SUBLANES = 8
MXU_N = 256
PAIR = LANES // HEAD_DIM
N_PAIRS = N_HEADS // PAIR
PAIRS_PER_KV = GROUP // PAIR
PAIR_STACK = 2
N_DENSE_UNITS = (2 * (D_ATTN // MXU_N - 1) + 4 * (D_CONV // MXU_N) + 2 * (D_MODEL // MXU_N)
                 + D_MODEL // MXU_N)
TQ = 1024
DOT_ROWS = 1024
EPILOGUE_ROWS = 256
EPILOGUE_SPLIT = tuple(range(0, TQ + 1, EPILOGUE_ROWS))
NEG = -1e30
VMEM_LIMIT = 62 * 1024 * 1024

assert D_KV == LANES and PAIR == 2 and PAIRS_PER_KV % PAIR_STACK == 0 and D_MODEL % MXU_N == 0
assert TQ % BLOCK == 0 and TQ % DOT_ROWS == 0 and TQ % EPILOGUE_ROWS == 0

_BF = jnp.bfloat16
_F32 = jnp.float32


def _dot(a, b):
    m = a.shape[0]
    if m <= DOT_ROWS:
        return jnp.dot(a, b, preferred_element_type=_F32)
    return jnp.concatenate(
        [jnp.dot(a[r:r + DOT_ROWS], b, preferred_element_type=_F32) for r in range(0, m, DOT_ROWS)],
        axis=0)


def _dot_nt(a, b):
    return lax.dot_general(a, b, (((1,), (1,)), ((), ())), preferred_element_type=_F32)


def _sigmoid(z):
    return 1.0 / (1.0 + jnp.exp(-z))


def _layer_kernel(sink_ref, x_ref, gpre_ref, gpost_ref, wconv_ref, tab_ref,
                  win_ref, wpc_ref, wpa_ref, wout_ref, o_ref,
                  h_scr, ua_scr, ub_scr, q_scr, sza_scr, k_scr, v_scr, ucarry, t_scr, sgb_scr):
    j = pl.program_id(1)

    @pl.when((pl.program_id(0) == 0) & (j == 0))
    def _():
        low = lax.broadcasted_iota(jnp.int32, (BLOCK + TQ, LANES), 1) < HEAD_DIM
        ones_lo = jnp.where(low, 1.0, 0.0).astype(_BF)
        ones_hi = jnp.where(low, 0.0, 1.0).astype(_BF)
        for i in range(2 * N_KV_HEADS):
            v_scr[i, :, LANES:] = ones_hi if i % 2 else ones_lo

    @pl.when(j == 0)
    def _():
        ucarry[...] = jnp.zeros_like(ucarry)
        k_scr[:, 0:BLOCK, :] = jnp.zeros((2 * N_KV_HEADS, BLOCK, LANES), _BF)
        v_scr[:, 0:BLOCK, 0:LANES] = jnp.zeros((2 * N_KV_HEADS, BLOCK, LANES), _BF)

    @pl.when(j != 0)
    def _():
        k_scr[:, 0:BLOCK, :] = k_scr[:, TQ:TQ + BLOCK, :]
        v_scr[:, 0:BLOCK, 0:LANES] = v_scr[:, TQ:TQ + BLOCK, 0:LANES]

    x = x_ref[0]
    ms = jnp.mean(x * x, axis=-1, keepdims=True)
    h_scr[...] = (x * lax.rsqrt(ms + RMS_EPS) * gpre_ref[...]).astype(_BF)

    def proj(off, n):
        lo = off + n * MXU_N
        return _dot(h_scr[...], win_ref[:, lo:lo + MXU_N])

    lane = lax.broadcasted_iota(jnp.int32, (TQ, LANES), 1)
    first_half = (lane & (HEAD_DIM // 2)) == 0
    low_head = lane < HEAD_DIM

    def rope(z, cos, sin):
        rot = jnp.where(first_half, pltpu.roll(z, LANES - HEAD_DIM // 2, 1),
                        pltpu.roll(z, HEAD_DIM // 2, 1))
        return z * cos + rot * sin

    def q_unit(n):
        qf = proj(OFF_Q, n) * (HEAD_DIM ** -0.5)
        for c in range(MXU_N // LANES):
            qc = rope(qf[:, c * LANES:(c + 1) * LANES], tab_ref[0], tab_ref[1])
            lo = n * MXU_N + c * LANES
            q_scr[:, lo:lo + LANES] = qc.astype(_BF)

    def za_unit(n):
        za = proj(OFF_ZA, n)
        sza_scr[:, n * MXU_N:(n + 1) * MXU_N] = (za * _sigmoid(za)).astype(_BF)

    row = lax.broadcasted_iota(jnp.int32, (TQ, MXU_N), 0)

    def dense_units():
        for n in range(1, D_ATTN // MXU_N):
            q_unit(n)
            yield
            za_unit(n)
            yield
        for n in range(D_CONV // MXU_N):
            cols = slice(n * MXU_N, (n + 1) * MXU_N)
            xc = proj(OFF_XC, n)
            yield
            u = proj(OFF_CG, n) * xc
            prev = ucarry[:, cols]
            p1 = prev[SUBLANES - 1:SUBLANES, :]
            p2 = prev[SUBLANES - 2:SUBLANES - 1, :]
            u1 = jnp.where(row == 0, p1, pltpu.roll(u, 1, 0))
            u2 = jnp.where(row == 0, p2, jnp.where(row == 1, p1, pltpu.roll(u, 2, 0)))
            ucarry[:, cols] = u[TQ - SUBLANES:, :]
            w = wconv_ref[:, cols]
            y = w[0:1, :] * u2 + w[1:2, :] * u1 + w[2:3, :] * u
            yield
            y = proj(OFF_BG, n) * y
            yield
            zc = proj(OFF_ZC, n)
            ua_scr[:, cols] = ((zc * _sigmoid(zc)) * y).astype(_BF)
            yield
        for n in range(D_MODEL // MXU_N):
            cols = slice(n * MXU_N, (n + 1) * MXU_N)
            ya = _dot(ua_scr[...], wpc_ref[:, cols])
            yield
            t_scr[:, cols] = _sigmoid(proj(OFF_GA, n)) * ya
            yield
        for n in range(D_MODEL // MXU_N):
            sgb_scr[:, n * MXU_N:(n + 1) * MXU_N] = _sigmoid(proj(OFF_GB, n)).astype(_BF)
            yield

    kv = proj(OFF_K, 0)
    k = rope(kv[:, :D_KV], tab_ref[0], tab_ref[1])
    v = kv[:, D_KV:]
    kr = pltpu.roll(k, HEAD_DIM, 1)
    vr = pltpu.roll(v, HEAD_DIM, 1)
    zero = jnp.zeros_like(k)
    k_variants = (jnp.where(low_head, k, zero), jnp.where(low_head, zero, kr),
                  jnp.where(low_head, kr, zero), jnp.where(low_head, zero, k))
    v_variants = (jnp.where(low_head, v, zero), jnp.where(low_head, zero, vr),
                  jnp.where(low_head, vr, zero), jnp.where(low_head, zero, v))
    for i in range(2 * N_KV_HEADS):
        k_scr[i, BLOCK:BLOCK + TQ, :] = k_variants[i].astype(_BF)
        v_scr[i, BLOCK:BLOCK + TQ, 0:LANES] = v_variants[i].astype(_BF)
    q_unit(0)
    za_unit(0)

    qi = lax.broadcasted_iota(jnp.int32, (BLOCK, 2 * BLOCK), 0)
    kj = lax.broadcasted_iota(jnp.int32, (BLOCK, 2 * BLOCK), 1)
    band = (kj > qi) & (kj <= qi + BLOCK)
    low_head_b = lax.broadcasted_iota(jnp.int32, (BLOCK, LANES), 1) < HEAD_DIM

    def softmax_parts(sc, mask, sink):
        sc = jnp.where(mask, sc, NEG)
        m = jnp.maximum(jnp.max(sc, axis=-1, keepdims=True), sink)
        return jnp.exp(sc - m).astype(_BF), jnp.exp(sink - m)

    dense = dense_units()
    next(dense)
    n_iters = (N_PAIRS // PAIR_STACK) * (TQ // BLOCK)
    n_left = N_DENSE_UNITS - 1
    it = 0
    for c0 in range(0, N_PAIRS, PAIR_STACK):
        kvh = c0 // PAIRS_PER_KV
        pairs = range(c0, c0 + PAIR_STACK)
        for b in range(TQ // BLOCK):
            r0 = b * BLOCK
            first_key = jnp.where(j == 0, BLOCK, 0) if b == 0 else 0
            mask = band & (kj >= first_key)
            qs = jnp.concatenate(
                [q_scr[r0:r0 + BLOCK, c * LANES:(c + 1) * LANES] for c in pairs], axis=0)
            ka = k_scr[2 * kvh, r0:r0 + 2 * BLOCK, :]
            kb = k_scr[2 * kvh + 1, r0:r0 + 2 * BLOCK, :]
            va = v_scr[2 * kvh, r0:r0 + 2 * BLOCK, :]
            vb = v_scr[2 * kvh + 1, r0:r0 + 2 * BLOCK, :]
            sa = _dot_nt(qs, ka)
            sb = _dot_nt(qs, kb)
            for _ in range(n_left * (it + 1) // n_iters - n_left * it // n_iters):
                next(dense)
            it += 1
            parts_a = [softmax_parts(sa[i * BLOCK:(i + 1) * BLOCK], mask, sink_ref[PAIR * c])
                       for i, c in enumerate(pairs)]
            parts_b = [softmax_parts(sb[i * BLOCK:(i + 1) * BLOCK], mask, sink_ref[PAIR * c + 1])
                       for i, c in enumerate(pairs)]
            pa = jnp.concatenate([p for p, _ in parts_a], axis=0)
            pb = jnp.concatenate([p for p, _ in parts_b], axis=0)
            ols = _dot(pa, va) + _dot(pb, vb)
            for i, c in enumerate(pairs):
                ol = ols[i * BLOCK:(i + 1) * BLOCK]
                denom = ol[:, LANES:] + jnp.where(low_head_b, parts_a[i][1], parts_b[i][1])
                gate = sza_scr[r0:r0 + BLOCK, c * LANES:(c + 1) * LANES]
                ub_scr[r0:r0 + BLOCK, c * LANES:(c + 1) * LANES] = (
                    gate * (ol[:, :LANES] * (1.0 / denom))).astype(_BF)
    for _ in dense:
        pass

    for r0, r1 in zip(EPILOGUE_SPLIT[:-1], EPILOGUE_SPLIT[1:]):
        rows = slice(r0, r1)
        for n in range(D_MODEL // MXU_N):
            cols = slice(n * MXU_N, (n + 1) * MXU_N)
            yb = _dot(ub_scr[rows, :], wpa_ref[:, cols])
            ua_scr[rows, cols] = (t_scr[rows, cols] + sgb_scr[rows, cols] * yb).astype(_BF)
        y = _dot(ua_scr[rows, :], wout_ref[...])
        ms2 = jnp.mean(y * y, axis=-1, keepdims=True)
        o_ref[0, rows, :] = x_ref[0, rows, :] + y * lax.rsqrt(ms2 + RMS_EPS) * gpost_ref[...]


def _rope_tables(t):
    inv_freq = ROPE_THETA ** (-jnp.arange(0, HEAD_DIM, 2, dtype=_F32) / HEAD_DIM)
    inv_freq = jnp.tile(inv_freq, 2 * PAIR)
    half = jnp.ones((HEAD_DIM // 2,), _F32)
    sign = jnp.tile(jnp.concatenate([-half, half]), PAIR)
    ang = jnp.arange(t).astype(_F32)[:, None] * inv_freq[None, :]
    return jnp.stack([jnp.cos(ang), jnp.sin(ang) * sign])


def _resident(shape):
    return pl.BlockSpec(shape, lambda b, j: (0,) * len(shape), pipeline_mode=pl.Buffered(1))


def _hybrid_layer(x, g_pre, g_post, w_in, w_conv, sinks, w_proj_conv, w_proj_attn, w_out):
    bsz, t, d = x.shape
    assert d == D_MODEL and t % TQ == 0 and w_in.shape == (D_MODEL, D_IN)
    tables = _rope_tables(t)
    return pl.pallas_call(
        _layer_kernel,
        out_shape=jax.ShapeDtypeStruct(x.shape, x.dtype),
        grid=(bsz, t // TQ),
        in_specs=[
            pl.BlockSpec(memory_space=pltpu.SMEM),
            pl.BlockSpec((1, TQ, D_MODEL), lambda b, j: (b, j, 0)),
            _resident((1, D_MODEL)),
            _resident((1, D_MODEL)),
            _resident((CONV_WIDTH, D_CONV)),
            pl.BlockSpec((2, TQ, LANES), lambda b, j: (0, j, 0)),
            _resident((D_MODEL, D_IN)),
            _resident((D_CONV, D_MODEL)),
            _resident((D_ATTN, D_MODEL)),
            _resident((D_MODEL, D_MODEL)),
        ],
        out_specs=pl.BlockSpec((1, TQ, D_MODEL), lambda b, j: (b, j, 0)),
        scratch_shapes=[
            pltpu.VMEM((TQ, D_MODEL), _BF),
            pltpu.VMEM((TQ, D_CONV), _BF),
            pltpu.VMEM((TQ, D_ATTN), _BF),
            pltpu.VMEM((TQ, D_ATTN), _BF),
            pltpu.VMEM((TQ, D_ATTN), _BF),
            pltpu.VMEM((2 * N_KV_HEADS, BLOCK + TQ, LANES), _BF),
            pltpu.VMEM((2 * N_KV_HEADS, BLOCK + TQ, 2 * LANES), _BF),
            pltpu.VMEM((SUBLANES, D_CONV), _F32),
            pltpu.VMEM((TQ, D_MODEL), _F32),
            pltpu.VMEM((TQ, D_MODEL), _BF),
        ],
        compiler_params=pltpu.CompilerParams(
            dimension_semantics=("arbitrary", "arbitrary"),
            vmem_limit_bytes=VMEM_LIMIT,
        ),
        name="hybrid_layer",
    )(sinks.astype(_F32), x, g_pre.reshape(1, D_MODEL), g_post.reshape(1, D_MODEL), w_conv,
      tables, w_in.astype(_BF), w_proj_conv.astype(_BF), w_proj_attn.astype(_BF),
      w_out.astype(_BF))


def kernel(x, g_pre, g_post, w_in, w_conv, sinks, w_proj_conv, w_proj_attn, w_out):
    for layer in range(g_pre.shape[0]):
        x = _hybrid_layer(x, g_pre[layer], g_post[layer], w_in[layer], w_conv[layer],
                          sinks[layer], w_proj_conv[layer], w_proj_attn[layer], w_out[layer])
    return x
```

```python
import jax
import jax.numpy as jnp
from jax import lax
from jax.experimental import pallas as pl
from jax.experimental.pallas import tpu as pltpu

D_MODEL = 1024
D_CONV = D_MODEL
CONV_WIDTH = 3
HEAD_DIM = 64
N_HEADS = D_MODEL // HEAD_DIM
N_KV_HEADS = 2
GROUP = N_HEADS // N_KV_HEADS
D_ATTN = N_HEADS * HEAD_DIM
D_KV = N_KV_HEADS * HEAD_DIM
WINDOW = 128
BLOCK = WINDOW
ROPE_THETA = 10000.0
RMS_EPS = 1e-6

OFF_XC = 0
OFF_BG = OFF_XC + D_CONV
OFF_CG = OFF_BG + D_CONV
OFF_ZC = OFF_CG + D_CONV
OFF_Q = OFF_ZC + D_CONV
OFF_K = OFF_Q + D_ATTN
OFF_V = OFF_K + D_KV
OFF_ZA = OFF_V + D_KV
OFF_GA = OFF_ZA + D_ATTN
OFF_GB = OFF_GA + D_MODEL
D_IN = OFF_GB + D_MODEL

LANES = 128
SUBLANES = 8
MXU_N = 256
PAIR = LANES // HEAD_DIM
N_PAIRS = N_HEADS // PAIR
PAIRS_PER_KV = GROUP // PAIR
PAIR_STACK = 2
N_DENSE_UNITS = (2 * (D_ATTN // MXU_N - 1) + 4 * (D_CONV // MXU_N) + 2 * (D_MODEL // MXU_N)
                 + D_MODEL // MXU_N)
TQ = 1024
DOT_ROWS = 512
EPILOGUE_ROWS = 256
EPILOGUE_SPLIT = tuple(range(0, TQ + 1, EPILOGUE_ROWS))
NEG = -1e30
VMEM_LIMIT = 62 * 1024 * 1024

assert D_KV == LANES and PAIR == 2 and PAIRS_PER_KV % PAIR_STACK == 0 and D_MODEL % MXU_N == 0
assert TQ % BLOCK == 0 and TQ % DOT_ROWS == 0 and TQ % EPILOGUE_ROWS == 0

_BF = jnp.bfloat16
_F32 = jnp.float32


def _dot(a, b):
    m = a.shape[0]
    if m <= DOT_ROWS:
        return jnp.dot(a, b, preferred_element_type=_F32)
    return jnp.concatenate(
        [jnp.dot(a[r:r + DOT_ROWS], b, preferred_element_type=_F32) for r in range(0, m, DOT_ROWS)],
        axis=0)


def _dot_nt(a, b):
    return lax.dot_general(a, b, (((1,), (1,)), ((), ())), preferred_element_type=_F32)


def _sigmoid(z):
    return 1.0 / (1.0 + jnp.exp(-z))


def _layer_kernel(sink_ref, x_ref, gpre_ref, gpost_ref, wconv_ref, tab_ref,
                  win_ref, wpc_ref, wpa_ref, wout_ref, o_ref,
                  h_scr, ua_scr, ub_scr, q_scr, sza_scr, k_scr, v_scr, ucarry, t_scr, sgb_scr):
    j = pl.program_id(1)

    @pl.when((pl.program_id(0) == 0) & (j == 0))
    def _():
        low = lax.broadcasted_iota(jnp.int32, (BLOCK + TQ, LANES), 1) < HEAD_DIM
        ones_lo = jnp.where(low, 1.0, 0.0).astype(_BF)
        ones_hi = jnp.where(low, 0.0, 1.0).astype(_BF)
        for i in range(2 * N_KV_HEADS):
            v_scr[i, :, LANES:] = ones_hi if i % 2 else ones_lo

    @pl.when(j == 0)
    def _():
        ucarry[...] = jnp.zeros_like(ucarry)
        k_scr[:, 0:BLOCK, :] = jnp.zeros((2 * N_KV_HEADS, BLOCK, LANES), _BF)
        v_scr[:, 0:BLOCK, 0:LANES] = jnp.zeros((2 * N_KV_HEADS, BLOCK, LANES), _BF)

    @pl.when(j != 0)
    def _():
        k_scr[:, 0:BLOCK, :] = k_scr[:, TQ:TQ + BLOCK, :]
        v_scr[:, 0:BLOCK, 0:LANES] = v_scr[:, TQ:TQ + BLOCK, 0:LANES]

    x = x_ref[0]
    ms = jnp.mean(x * x, axis=-1, keepdims=True)
    h_scr[...] = (x * lax.rsqrt(ms + RMS_EPS) * gpre_ref[...]).astype(_BF)

    def proj(off, n):
        lo = off + n * MXU_N
        return _dot(h_scr[...], win_ref[:, lo:lo + MXU_N])

    lane = lax.broadcasted_iota(jnp.int32, (TQ, LANES), 1)
    first_half = (lane & (HEAD_DIM // 2)) == 0
    low_head = lane < HEAD_DIM

    def rope(z, cos, sin):
        rot = jnp.where(first_half, pltpu.roll(z, LANES - HEAD_DIM // 2, 1),
                        pltpu.roll(z, HEAD_DIM // 2, 1))
        return z * cos + rot * sin

    def q_unit(n):
        qf = proj(OFF_Q, n) * (HEAD_DIM ** -0.5)
        for c in range(MXU_N // LANES):
            qc = rope(qf[:, c * LANES:(c + 1) * LANES], tab_ref[0], tab_ref[1])
            lo = n * MXU_N + c * LANES
            q_scr[:, lo:lo + LANES] = qc.astype(_BF)

    def za_unit(n):
        za = proj(OFF_ZA, n)
        sza_scr[:, n * MXU_N:(n + 1) * MXU_N] = (za * _sigmoid(za)).astype(_BF)

    row = lax.broadcasted_iota(jnp.int32, (TQ, MXU_N), 0)

    def dense_units():
        for n in range(1, D_ATTN // MXU_N):
            q_unit(n)
            yield
            za_unit(n)
            yield
        for n in range(D_CONV // MXU_N):
            cols = slice(n * MXU_N, (n + 1) * MXU_N)
            xc = proj(OFF_XC, n)
            yield
            u = proj(OFF_CG, n) * xc
            prev = ucarry[:, cols]
            p1 = prev[SUBLANES - 1:SUBLANES, :]
            p2 = prev[SUBLANES - 2:SUBLANES - 1, :]
            u1 = jnp.where(row == 0, p1, pltpu.roll(u, 1, 0))
            u2 = jnp.where(row == 0, p2, jnp.where(row == 1, p1, pltpu.roll(u, 2, 0)))
            ucarry[:, cols] = u[TQ - SUBLANES:, :]
            w = wconv_ref[:, cols]
            y = w[0:1, :] * u2 + w[1:2, :] * u1 + w[2:3, :] * u
            yield
            y = proj(OFF_BG, n) * y
            yield
            zc = proj(OFF_ZC, n)
            ua_scr[:, cols] = ((zc * _sigmoid(zc)) * y).astype(_BF)
            yield
        for n in range(D_MODEL // MXU_N):
            cols = slice(n * MXU_N, (n + 1) * MXU_N)
            ya = _dot(ua_scr[...], wpc_ref[:, cols])
            yield
            t_scr[:, cols] = _sigmoid(proj(OFF_GA, n)) * ya
            yield
        for n in range(D_MODEL // MXU_N):
            sgb_scr[:, n * MXU_N:(n + 1) * MXU_N] = _sigmoid(proj(OFF_GB, n)).astype(_BF)
            yield

    kv = proj(OFF_K, 0)
    k = rope(kv[:, :D_KV], tab_ref[0], tab_ref[1])
    v = kv[:, D_KV:]
    kr = pltpu.roll(k, HEAD_DIM, 1)
    vr = pltpu.roll(v, HEAD_DIM, 1)
    zero = jnp.zeros_like(k)
    k_variants = (jnp.where(low_head, k, zero), jnp.where(low_head, zero, kr),
                  jnp.where(low_head, kr, zero), jnp.where(low_head, zero, k))
    v_variants = (jnp.where(low_head, v, zero), jnp.where(low_head, zero, vr),
                  jnp.where(low_head, vr, zero), jnp.where(low_head, zero, v))
    for i in range(2 * N_KV_HEADS):
        k_scr[i, BLOCK:BLOCK + TQ, :] = k_variants[i].astype(_BF)
        v_scr[i, BLOCK:BLOCK + TQ, 0:LANES] = v_variants[i].astype(_BF)
    q_unit(0)
    za_unit(0)

    qi = lax.broadcasted_iota(jnp.int32, (BLOCK, 2 * BLOCK), 0)
    kj = lax.broadcasted_iota(jnp.int32, (BLOCK, 2 * BLOCK), 1)
    band = (kj > qi) & (kj <= qi + BLOCK)
    low_head_b = lax.broadcasted_iota(jnp.int32, (BLOCK, LANES), 1) < HEAD_DIM

    def softmax_parts(sc, mask, sink):
        sc = jnp.where(mask, sc, NEG)
        m = jnp.maximum(jnp.max(sc, axis=-1, keepdims=True), sink)
        return jnp.exp(sc - m).astype(_BF), jnp.exp(sink - m)

    dense = dense_units()
    n_iters = (N_PAIRS // PAIR_STACK) * (TQ // BLOCK)
    n_left = N_DENSE_UNITS
    it = 0
    for c0 in range(0, N_PAIRS, PAIR_STACK):
        kvh = c0 // PAIRS_PER_KV
        pairs = range(c0, c0 + PAIR_STACK)
        for b in range(TQ // BLOCK):
            r0 = b * BLOCK
            first_key = jnp.where(j == 0, BLOCK, 0) if b == 0 else 0
            mask = band & (kj >= first_key)
            qs = jnp.concatenate(
                [q_scr[r0:r0 + BLOCK, c * LANES:(c + 1) * LANES] for c in pairs], axis=0)
            ka = k_scr[2 * kvh, r0:r0 + 2 * BLOCK, :]
            kb = k_scr[2 * kvh + 1, r0:r0 + 2 * BLOCK, :]
            va = v_scr[2 * kvh, r0:r0 + 2 * BLOCK, :]
            vb = v_scr[2 * kvh + 1, r0:r0 + 2 * BLOCK, :]
            sa = _dot_nt(qs, ka)
            sb = _dot_nt(qs, kb)
            for _ in range(n_left * (it + 1) // n_iters - n_left * it // n_iters):
                next(dense)
            it += 1
            parts_a = [softmax_parts(sa[i * BLOCK:(i + 1) * BLOCK], mask, sink_ref[PAIR * c])
                       for i, c in enumerate(pairs)]
            parts_b = [softmax_parts(sb[i * BLOCK:(i + 1) * BLOCK], mask, sink_ref[PAIR * c + 1])
                       for i, c in enumerate(pairs)]
            pa = jnp.concatenate([p for p, _ in parts_a], axis=0)
            pb = jnp.concatenate([p for p, _ in parts_b], axis=0)
            ols = _dot(pa, va) + _dot(pb, vb)
            for i, c in enumerate(pairs):
                ol = ols[i * BLOCK:(i + 1) * BLOCK]
                denom = ol[:, LANES:] + jnp.where(low_head_b, parts_a[i][1], parts_b[i][1])
                gate = sza_scr[r0:r0 + BLOCK, c * LANES:(c + 1) * LANES]
                ub_scr[r0:r0 + BLOCK, c * LANES:(c + 1) * LANES] = (
                    gate * (ol[:, :LANES] * (1.0 / denom))).astype(_BF)
    for _ in dense:
        pass

    for r0, r1 in zip(EPILOGUE_SPLIT[:-1], EPILOGUE_SPLIT[1:]):
        rows = slice(r0, r1)
        for n in range(D_MODEL // MXU_N):
            cols = slice(n * MXU_N, (n + 1) * MXU_N)
            yb = _dot(ub_scr[rows, :], wpa_ref[:, cols])
            ua_scr[rows, cols] = (t_scr[rows, cols] + sgb_scr[rows, cols] * yb).astype(_BF)
        y = _dot(ua_scr[rows, :], wout_ref[...])
        ms2 = jnp.mean(y * y, axis=-1, keepdims=True)
        o_ref[0, rows, :] = x_ref[0, rows, :] + y * lax.rsqrt(ms2 + RMS_EPS) * gpost_ref[...]


def _rope_tables(t):
    inv_freq = ROPE_THETA ** (-jnp.arange(0, HEAD_DIM, 2, dtype=_F32) / HEAD_DIM)
    inv_freq = jnp.tile(inv_freq, 2 * PAIR)
    half = jnp.ones((HEAD_DIM // 2,), _F32)
    sign = jnp.tile(jnp.concatenate([-half, half]), PAIR)
    ang = jnp.arange(t).astype(_F32)[:, None] * inv_freq[None, :]
    return jnp.stack([jnp.cos(ang), jnp.sin(ang) * sign])


def _resident(shape):
    return pl.BlockSpec(shape, lambda b, j: (0,) * len(shape), pipeline_mode=pl.Buffered(1))


def _hybrid_layer(x, g_pre, g_post, w_in, w_conv, sinks, w_proj_conv, w_proj_attn, w_out):
    bsz, t, d = x.shape
    assert d == D_MODEL and t % TQ == 0 and w_in.shape == (D_MODEL, D_IN)
    tables = _rope_tables(t)
    return pl.pallas_call(
        _layer_kernel,
        out_shape=jax.ShapeDtypeStruct(x.shape, x.dtype),
        grid=(bsz, t // TQ),
        in_specs=[
            pl.BlockSpec(memory_space=pltpu.SMEM),
            pl.BlockSpec((1, TQ, D_MODEL), lambda b, j: (b, j, 0)),
            _resident((1, D_MODEL)),
            _resident((1, D_MODEL)),
            _resident((CONV_WIDTH, D_CONV)),
            pl.BlockSpec((2, TQ, LANES), lambda b, j: (0, j, 0)),
            _resident((D_MODEL, D_IN)),
            _resident((D_CONV, D_MODEL)),
            _resident((D_ATTN, D_MODEL)),
            _resident((D_MODEL, D_MODEL)),
        ],
        out_specs=pl.BlockSpec((1, TQ, D_MODEL), lambda b, j: (b, j, 0)),
        scratch_shapes=[
            pltpu.VMEM((TQ, D_MODEL), _BF),
            pltpu.VMEM((TQ, D_CONV), _BF),
            pltpu.VMEM((TQ, D_ATTN), _BF),
            pltpu.VMEM((TQ, D_ATTN), _BF),
            pltpu.VMEM((TQ, D_ATTN), _BF),
            pltpu.VMEM((2 * N_KV_HEADS, BLOCK + TQ, LANES), _BF),
            pltpu.VMEM((2 * N_KV_HEADS, BLOCK + TQ, 2 * LANES), _BF),
            pltpu.VMEM((SUBLANES, D_CONV), _F32),
            pltpu.VMEM((TQ, D_MODEL), _F32),
            pltpu.VMEM((TQ, D_MODEL), _BF),
        ],
        compiler_params=pltpu.CompilerParams(
            dimension_semantics=("arbitrary", "arbitrary"),
            vmem_limit_bytes=VMEM_LIMIT,
        ),
        name="hybrid_layer",
    )(sinks.astype(_F32), x, g_pre.reshape(1, D_MODEL), g_post.reshape(1, D_MODEL), w_conv,
      tables, w_in.astype(_BF), w_proj_conv.astype(_BF), w_proj_attn.astype(_BF),
      w_out.astype(_BF))


def kernel(x, g_pre, g_post, w_in, w_conv, sinks, w_proj_conv, w_proj_attn, w_out):
    for layer in range(g_pre.shape[0]):
        x = _hybrid_layer(x, g_pre[layer], g_post[layer], w_in[layer], w_conv[layer],
                          sinks[layer], w_proj_conv[layer], w_proj_attn[layer], w_out[layer])
    return x
```

```python
import jax
import jax.numpy as jnp
from jax import lax
from jax.experimental import pallas as pl
from jax.experimental.pallas import tpu as pltpu

D_MODEL = 1024
D_CONV = D_MODEL
CONV_WIDTH = 3
HEAD_DIM = 64
N_HEADS = D_MODEL // HEAD_DIM
N_KV_HEADS = 2
GROUP = N_HEADS // N_KV_HEADS
D_ATTN = N_HEADS * HEAD_DIM
D_KV = N_KV_HEADS * HEAD_DIM
WINDOW = 128
BLOCK = WINDOW
ROPE_THETA = 10000.0
RMS_EPS = 1e-6

OFF_XC = 0
OFF_BG = OFF_XC + D_CONV
OFF_CG = OFF_BG + D_CONV
OFF_ZC = OFF_CG + D_CONV
OFF_Q = OFF_ZC + D_CONV
OFF_K = OFF_Q + D_ATTN
OFF_V = OFF_K + D_KV
OFF_ZA = OFF_V + D_KV
OFF_GA = OFF_ZA + D_ATTN
OFF_GB = OFF_GA + D_MODEL
D_IN = OFF_GB + D_MODEL

LANES = 128
SUBLANES = 8
MXU_N = 256
PAIR = LANES // HEAD_DIM
N_PAIRS = N_HEADS // PAIR
PAIRS_PER_KV = GROUP // PAIR
PAIR_STACK = 2
N_DENSE_UNITS = (2 * (D_ATTN // MXU_N) - 1 + 4 * (D_CONV // MXU_N) + 2 * (D_MODEL // MXU_N)
                 + D_MODEL // MXU_N)
TQ = 1024
DOT_ROWS = 512
EPILOGUE_ROWS = 256
EPILOGUE_SPLIT = tuple(range(0, TQ + 1, EPILOGUE_ROWS))
NEG = -1e30
VMEM_LIMIT = 62 * 1024 * 1024

assert D_KV == LANES and PAIR == 2 and PAIRS_PER_KV % PAIR_STACK == 0 and D_MODEL % MXU_N == 0
assert TQ % BLOCK == 0 and TQ % DOT_ROWS == 0 and TQ % EPILOGUE_ROWS == 0

_BF = jnp.bfloat16
_F32 = jnp.float32


def _dot(a, b):
    m = a.shape[0]
    if m <= DOT_ROWS:
        return jnp.dot(a, b, preferred_element_type=_F32)
    return jnp.concatenate(
        [jnp.dot(a[r:r + DOT_ROWS], b, preferred_element_type=_F32) for r in range(0, m, DOT_ROWS)],
        axis=0)


def _dot_nt(a, b):
    return lax.dot_general(a, b, (((1,), (1,)), ((), ())), preferred_element_type=_F32)


def _sigmoid(z):
    return 1.0 / (1.0 + jnp.exp(-z))


def _layer_kernel(sink_ref, x_ref, gpre_ref, gpost_ref, wconv_ref, tab_ref,
                  win_ref, wpc_ref, wpa_ref, wout_ref, o_ref,
                  h_scr, ua_scr, ub_scr, q_scr, sza_scr, k_scr, v_scr, ucarry, t_scr, sgb_scr):
    j = pl.program_id(1)

    @pl.when((pl.program_id(0) == 0) & (j == 0))
    def _():
        low = lax.broadcasted_iota(jnp.int32, (BLOCK + TQ, LANES), 1) < HEAD_DIM
        ones_lo = jnp.where(low, 1.0, 0.0).astype(_BF)
        ones_hi = jnp.where(low, 0.0, 1.0).astype(_BF)
        for i in range(2 * N_KV_HEADS):
            v_scr[i, :, LANES:] = ones_hi if i % 2 else ones_lo

    @pl.when(j == 0)
    def _():
        ucarry[...] = jnp.zeros_like(ucarry)
        k_scr[:, 0:BLOCK, :] = jnp.zeros((2 * N_KV_HEADS, BLOCK, LANES), _BF)
        v_scr[:, 0:BLOCK, 0:LANES] = jnp.zeros((2 * N_KV_HEADS, BLOCK, LANES), _BF)

    @pl.when(j != 0)
    def _():
        k_scr[:, 0:BLOCK, :] = k_scr[:, TQ:TQ + BLOCK, :]
        v_scr[:, 0:BLOCK, 0:LANES] = v_scr[:, TQ:TQ + BLOCK, 0:LANES]

    x = x_ref[0]
    ms = jnp.mean(x * x, axis=-1, keepdims=True)
    h_scr[...] = (x * lax.rsqrt(ms + RMS_EPS) * gpre_ref[...]).astype(_BF)

    def proj(off, n):
        lo = off + n * MXU_N
        return _dot(h_scr[...], win_ref[:, lo:lo + MXU_N])

    lane = lax.broadcasted_iota(jnp.int32, (TQ, LANES), 1)
    first_half = (lane & (HEAD_DIM // 2)) == 0
    low_head = lane < HEAD_DIM

    def rope(z, cos, sin):
        rot = jnp.where(first_half, pltpu.roll(z, LANES - HEAD_DIM // 2, 1),
                        pltpu.roll(z, HEAD_DIM // 2, 1))
        return z * cos + rot * sin

    def q_unit(n):
        qf = proj(OFF_Q, n) * (HEAD_DIM ** -0.5)
        for c in range(MXU_N // LANES):
            qc = rope(qf[:, c * LANES:(c + 1) * LANES], tab_ref[0], tab_ref[1])
            lo = n * MXU_N + c * LANES
            q_scr[:, lo:lo + LANES] = qc.astype(_BF)

    def za_unit(n):
        za = proj(OFF_ZA, n)
        sza_scr[:, n * MXU_N:(n + 1) * MXU_N] = (za * _sigmoid(za)).astype(_BF)

    row = lax.broadcasted_iota(jnp.int32, (TQ, MXU_N), 0)

    def dense_units():
        za_unit(0)
        yield
        for n in range(1, D_ATTN // MXU_N):
            q_unit(n)
            yield
            za_unit(n)
            yield
        for n in range(D_CONV // MXU_N):
            cols = slice(n * MXU_N, (n + 1) * MXU_N)
            xc = proj(OFF_XC, n)
            yield
            u = proj(OFF_CG, n) * xc
            prev = ucarry[:, cols]
            p1 = prev[SUBLANES - 1:SUBLANES, :]
            p2 = prev[SUBLANES - 2:SUBLANES - 1, :]
            u1 = jnp.where(row == 0, p1, pltpu.roll(u, 1, 0))
            u2 = jnp.where(row == 0, p2, jnp.where(row == 1, p1, pltpu.roll(u, 2, 0)))
            ucarry[:, cols] = u[TQ - SUBLANES:, :]
            w = wconv_ref[:, cols]
            y = w[0:1, :] * u2 + w[1:2, :] * u1 + w[2:3, :] * u
            yield
            y = proj(OFF_BG, n) * y
            yield
            zc = proj(OFF_ZC, n)
            ua_scr[:, cols] = ((zc * _sigmoid(zc)) * y).astype(_BF)
            yield
        for n in range(D_MODEL // MXU_N):
            cols = slice(n * MXU_N, (n + 1) * MXU_N)
            ya = _dot(ua_scr[...], wpc_ref[:, cols])
            yield
            t_scr[:, cols] = _sigmoid(proj(OFF_GA, n)) * ya
            yield
        for n in range(D_MODEL // MXU_N):
            sgb_scr[:, n * MXU_N:(n + 1) * MXU_N] = _sigmoid(proj(OFF_GB, n)).astype(_BF)
            yield

    kv = proj(OFF_K, 0)
    k = rope(kv[:, :D_KV], tab_ref[0], tab_ref[1])
    v = kv[:, D_KV:]
    kr = pltpu.roll(k, HEAD_DIM, 1)
    vr = pltpu.roll(v, HEAD_DIM, 1)
    zero = jnp.zeros_like(k)
    k_variants = (jnp.where(low_head, k, zero), jnp.where(low_head, zero, kr),
                  jnp.where(low_head, kr, zero), jnp.where(low_head, zero, k))
    v_variants = (jnp.where(low_head, v, zero), jnp.where(low_head, zero, vr),
                  jnp.where(low_head, vr, zero), jnp.where(low_head, zero, v))
    for i in range(2 * N_KV_HEADS):
        k_scr[i, BLOCK:BLOCK + TQ, :] = k_variants[i].astype(_BF)
        v_scr[i, BLOCK:BLOCK + TQ, 0:LANES] = v_variants[i].astype(_BF)
    q_unit(0)

    qi = lax.broadcasted_iota(jnp.int32, (BLOCK, 2 * BLOCK), 0)
    kj = lax.broadcasted_iota(jnp.int32, (BLOCK, 2 * BLOCK), 1)
    band = (kj > qi) & (kj <= qi + BLOCK)
    low_head_b = lax.broadcasted_iota(jnp.int32, (BLOCK, LANES), 1) < HEAD_DIM

    def softmax_parts(sc, mask, sink):
        sc = jnp.where(mask, sc, NEG)
        m = jnp.maximum(jnp.max(sc, axis=-1, keepdims=True), sink)
        return jnp.exp(sc - m).astype(_BF), jnp.exp(sink - m)

    dense = dense_units()
    n_iters = (N_PAIRS // PAIR_STACK) * (TQ // BLOCK)
    n_left = N_DENSE_UNITS
    it = 0
    for c0 in range(0, N_PAIRS, PAIR_STACK):
        kvh = c0 // PAIRS_PER_KV
        pairs = range(c0, c0 + PAIR_STACK)
        for b in range(TQ // BLOCK):
            r0 = b * BLOCK
            first_key = jnp.where(j == 0, BLOCK, 0) if b == 0 else 0
            mask = band & (kj >= first_key)
            qs = jnp.concatenate(
                [q_scr[r0:r0 + BLOCK, c * LANES:(c + 1) * LANES] for c in pairs], axis=0)
            ka = k_scr[2 * kvh, r0:r0 + 2 * BLOCK, :]
            kb = k_scr[2 * kvh + 1, r0:r0 + 2 * BLOCK, :]
            va = v_scr[2 * kvh, r0:r0 + 2 * BLOCK, :]
            vb = v_scr[2 * kvh + 1, r0:r0 + 2 * BLOCK, :]
            sa = _dot_nt(qs, ka)
            sb = _dot_nt(qs, kb)
            for _ in range(n_left * (it + 1) // n_iters - n_left * it // n_iters):
                next(dense)
            it += 1
            parts_a = [softmax_parts(sa[i * BLOCK:(i + 1) * BLOCK], mask, sink_ref[PAIR * c])
                       for i, c in enumerate(pairs)]
            parts_b = [softmax_parts(sb[i * BLOCK:(i + 1) * BLOCK], mask, sink_ref[PAIR * c + 1])
                       for i, c in enumerate(pairs)]
            pa = jnp.concatenate([p for p, _ in parts_a], axis=0)
            pb = jnp.concatenate([p for p, _ in parts_b], axis=0)
            ols = _dot(pa, va) + _dot(pb, vb)
            for i, c in enumerate(pairs):
                ol = ols[i * BLOCK:(i + 1) * BLOCK]
                denom = ol[:, LANES:] + jnp.where(low_head_b, parts_a[i][1], parts_b[i][1])
                gate = sza_scr[r0:r0 + BLOCK, c * LANES:(c + 1) * LANES]
                ub_scr[r0:r0 + BLOCK, c * LANES:(c + 1) * LANES] = (
                    gate * (ol[:, :LANES] * (1.0 / denom))).astype(_BF)
    for _ in dense:
        pass

    for r0, r1 in zip(EPILOGUE_SPLIT[:-1], EPILOGUE_SPLIT[1:]):
        rows = slice(r0, r1)
        for n in range(D_MODEL // MXU_N):
            cols = slice(n * MXU_N, (n + 1) * MXU_N)
            yb = _dot(ub_scr[rows, :], wpa_ref[:, cols])
            ua_scr[rows, cols] = (t_scr[rows, cols] + sgb_scr[rows, cols] * yb).astype(_BF)
        y = _dot(ua_scr[rows, :], wout_ref[...])
        ms2 = jnp.mean(y * y, axis=-1, keepdims=True)
        o_ref[0, rows, :] = x_ref[0, rows, :] + y * lax.rsqrt(ms2 + RMS_EPS) * gpost_ref[...]


def _rope_tables(t):
    inv_freq = ROPE_THETA ** (-jnp.arange(0, HEAD_DIM, 2, dtype=_F32) / HEAD_DIM)
    inv_freq = jnp.tile(inv_freq, 2 * PAIR)
    half = jnp.ones((HEAD_DIM // 2,), _F32)
    sign = jnp.tile(jnp.concatenate([-half, half]), PAIR)
    ang = jnp.arange(t).astype(_F32)[:, None] * inv_freq[None, :]
    return jnp.stack([jnp.cos(ang), jnp.sin(ang) * sign])


def _resident(shape):
    return pl.BlockSpec(shape, lambda b, j: (0,) * len(shape), pipeline_mode=pl.Buffered(1))


def _hybrid_layer(x, g_pre, g_post, w_in, w_conv, sinks, w_proj_conv, w_proj_attn, w_out):
    bsz, t, d = x.shape
    assert d == D_MODEL and t % TQ == 0 and w_in.shape == (D_MODEL, D_IN)
    tables = _rope_tables(t)
    return pl.pallas_call(
        _layer_kernel,
        out_shape=jax.ShapeDtypeStruct(x.shape, x.dtype),
        grid=(bsz, t // TQ),
        in_specs=[
            pl.BlockSpec(memory_space=pltpu.SMEM),
            pl.BlockSpec((1, TQ, D_MODEL), lambda b, j: (b, j, 0)),
            _resident((1, D_MODEL)),
            _resident((1, D_MODEL)),
            _resident((CONV_WIDTH, D_CONV)),
            pl.BlockSpec((2, TQ, LANES), lambda b, j: (0, j, 0)),
            _resident((D_MODEL, D_IN)),
            _resident((D_CONV, D_MODEL)),
            _resident((D_ATTN, D_MODEL)),
            _resident((D_MODEL, D_MODEL)),
        ],
        out_specs=pl.BlockSpec((1, TQ, D_MODEL), lambda b, j: (b, j, 0)),
        scratch_shapes=[
            pltpu.VMEM((TQ, D_MODEL), _BF),
            pltpu.VMEM((TQ, D_CONV), _BF),
            pltpu.VMEM((TQ, D_ATTN), _BF),
            pltpu.VMEM((TQ, D_ATTN), _BF),
            pltpu.VMEM((TQ, D_ATTN), _BF),
            pltpu.VMEM((2 * N_KV_HEADS, BLOCK + TQ, LANES), _BF),
            pltpu.VMEM((2 * N_KV_HEADS, BLOCK + TQ, 2 * LANES), _BF),
            pltpu.VMEM((SUBLANES, D_CONV), _F32),
            pltpu.VMEM((TQ, D_MODEL), _F32),
            pltpu.VMEM((TQ, D_MODEL), _BF),
        ],
        compiler_params=pltpu.CompilerParams(
            dimension_semantics=("arbitrary", "arbitrary"),
            vmem_limit_bytes=VMEM_LIMIT,
        ),
        name="hybrid_layer",
    )(sinks.astype(_F32), x, g_pre.reshape(1, D_MODEL), g_post.reshape(1, D_MODEL), w_conv,
      tables, w_in.astype(_BF), w_proj_conv.astype(_BF), w_proj_attn.astype(_BF),
      w_out.astype(_BF))


def kernel(x, g_pre, g_post, w_in, w_conv, sinks, w_proj_conv, w_proj_attn, w_out):
    for layer in range(g_pre.shape[0]):
        x = _hybrid_layer(x, g_pre[layer], g_post[layer], w_in[layer], w_conv[layer],
                          sinks[layer], w_proj_conv[layer], w_proj_attn[layer], w_out[layer])
    return x
```

```python
import jax
import jax.numpy as jnp
from jax import lax
from jax.experimental import pallas as pl
from jax.experimental.pallas import tpu as pltpu

D_MODEL = 1024
D_CONV = D_MODEL
CONV_WIDTH = 3
HEAD_DIM = 64
N_HEADS = D_MODEL // HEAD_DIM
N_KV_HEADS = 2
GROUP = N_HEADS // N_KV_HEADS
D_ATTN = N_HEADS * HEAD_DIM
D_KV = N_KV_HEADS * HEAD_DIM
WINDOW = 128
BLOCK = WINDOW
ROPE_THETA = 10000.0
RMS_EPS = 1e-6

OFF_XC = 0
OFF_BG = OFF_XC + D_CONV
OFF_CG = OFF_BG + D_CONV
OFF_ZC = OFF_CG + D_CONV
OFF_Q = OFF_ZC + D_CONV
OFF_K = OFF_Q + D_ATTN
OFF_V = OFF_K + D_KV
OFF_ZA = OFF_V + D_KV
OFF_GA = OFF_ZA + D_ATTN
OFF_GB = OFF_GA + D_MODEL
D_IN = OFF_GB + D_MODEL

LANES = 128
SUBLANES = 8
MXU_N = 256
PAIR = LANES // HEAD_DIM
N_PAIRS = N_HEADS // PAIR
PAIRS_PER_KV = GROUP // PAIR
PAIR_STACK = 2
N_DENSE_UNITS = (2 * (D_ATTN // MXU_N - 1) + 4 * (D_CONV // MXU_N) + 2 * (D_MODEL // MXU_N)
                 + D_MODEL // MXU_N)
TQ = 1024
DOT_ROWS = 512
EPILOGUE_ROWS = 256
EPILOGUE_SPLIT = tuple(range(0, TQ + 1, EPILOGUE_ROWS))
NEG = -1e30
VMEM_LIMIT = 62 * 1024 * 1024

assert D_KV == LANES and PAIR == 2 and PAIRS_PER_KV % PAIR_STACK == 0 and D_MODEL % MXU_N == 0
assert TQ % BLOCK == 0 and TQ % DOT_ROWS == 0 and TQ % EPILOGUE_ROWS == 0

_BF = jnp.bfloat16
_F32 = jnp.float32


def _dot(a, b):
    m = a.shape[0]
    if m <= DOT_ROWS:
        return jnp.dot(a, b, preferred_element_type=_F32)
    return jnp.concatenate(
        [jnp.dot(a[r:r + DOT_ROWS], b, preferred_element_type=_F32) for r in range(0, m, DOT_ROWS)],
        axis=0)


def _dot_nt(a, b):
    return lax.dot_general(a, b, (((1,), (1,)), ((), ())), preferred_element_type=_F32)


def _sigmoid(z):
    return 1.0 / (1.0 + jnp.exp(-z))


def _layer_kernel(sink_ref, x_ref, gpre_ref, gpost_ref, wconv_ref, tab_ref,
                  win_ref, wpc_ref, wpa_ref, wout_ref, o_ref,
                  h_scr, ua_scr, ub_scr, q_scr, sza_scr, k_scr, v_scr, ucarry, t_scr, sgb_scr):
    j = pl.program_id(1)

    @pl.when((pl.program_id(0) == 0) & (j == 0))
    def _():
        low = lax.broadcasted_iota(jnp.int32, (BLOCK + TQ, LANES), 1) < HEAD_DIM
        ones_lo = jnp.where(low, 1.0, 0.0).astype(_BF)
        ones_hi = jnp.where(low, 0.0, 1.0).astype(_BF)
        for i in range(2 * N_KV_HEADS):
            v_scr[i, :, LANES:] = ones_hi if i % 2 else ones_lo

    @pl.when(j == 0)
    def _():
        ucarry[...] = jnp.zeros_like(ucarry)
        k_scr[:, 0:BLOCK, :] = jnp.zeros((2 * N_KV_HEADS, BLOCK, LANES), _BF)
        v_scr[:, 0:BLOCK, 0:LANES] = jnp.zeros((2 * N_KV_HEADS, BLOCK, LANES), _BF)

    @pl.when(j != 0)
    def _():
        k_scr[:, 0:BLOCK, :] = k_scr[:, TQ:TQ + BLOCK, :]
        v_scr[:, 0:BLOCK, 0:LANES] = v_scr[:, TQ:TQ + BLOCK, 0:LANES]

    x = x_ref[0]
    ms = jnp.mean(x * x, axis=-1, keepdims=True)
    h_scr[...] = (x * lax.rsqrt(ms + RMS_EPS) * gpre_ref[...]).astype(_BF)

    def proj(off, n):
        lo = off + n * MXU_N
        return _dot(h_scr[...], win_ref[:, lo:lo + MXU_N])

    lane = lax.broadcasted_iota(jnp.int32, (TQ, LANES), 1)
    first_half = (lane & (HEAD_DIM // 2)) == 0
    low_head = lane < HEAD_DIM

    def rope(z, cos, sin):
        rot = jnp.where(first_half, pltpu.roll(z, LANES - HEAD_DIM // 2, 1),
                        pltpu.roll(z, HEAD_DIM // 2, 1))
        return z * cos + rot * sin

    def q_unit(n):
        qf = proj(OFF_Q, n) * (HEAD_DIM ** -0.5)
        for c in range(MXU_N // LANES):
            qc = rope(qf[:, c * LANES:(c + 1) * LANES], tab_ref[0], tab_ref[1])
            lo = n * MXU_N + c * LANES
            q_scr[:, lo:lo + LANES] = qc.astype(_BF)

    def za_unit(n):
        za = proj(OFF_ZA, n)
        sza_scr[:, n * MXU_N:(n + 1) * MXU_N] = (za * _sigmoid(za)).astype(_BF)

    row = lax.broadcasted_iota(jnp.int32, (TQ, MXU_N), 0)

    def dense_units():
        for n in range(1, D_ATTN // MXU_N):
            q_unit(n)
            yield
            za_unit(n)
            yield
        for n in range(D_CONV // MXU_N):
            cols = slice(n * MXU_N, (n + 1) * MXU_N)
            xc = proj(OFF_XC, n)
            yield
            u = proj(OFF_CG, n) * xc
            prev = ucarry[:, cols]
            p1 = prev[SUBLANES - 1:SUBLANES, :]
            p2 = prev[SUBLANES - 2:SUBLANES - 1, :]
            u1 = jnp.where(row == 0, p1, pltpu.roll(u, 1, 0))
            u2 = jnp.where(row == 0, p2, jnp.where(row == 1, p1, pltpu.roll(u, 2, 0)))
            ucarry[:, cols] = u[TQ - SUBLANES:, :]
            w = wconv_ref[:, cols]
            y = w[0:1, :] * u2 + w[1:2, :] * u1 + w[2:3, :] * u
            yield
            y = proj(OFF_BG, n) * y
            yield
            zc = proj(OFF_ZC, n)
            ua_scr[:, cols] = ((zc * _sigmoid(zc)) * y).astype(_BF)
            yield
        for n in range(D_MODEL // MXU_N):
            cols = slice(n * MXU_N, (n + 1) * MXU_N)
            ya = _dot(ua_scr[...], wpc_ref[:, cols])
            yield
            t_scr[:, cols] = _sigmoid(proj(OFF_GA, n)) * ya
            yield
        for n in range(D_MODEL // MXU_N):
            sgb_scr[:, n * MXU_N:(n + 1) * MXU_N] = _sigmoid(proj(OFF_GB, n)).astype(_BF)
            yield

    kv = proj(OFF_K, 0)
    k = rope(kv[:, :D_KV], tab_ref[0], tab_ref[1])
    v = kv[:, D_KV:]
    kr = pltpu.roll(k, HEAD_DIM, 1)
    vr = pltpu.roll(v, HEAD_DIM, 1)
    zero = jnp.zeros_like(k)
    k_variants = (jnp.where(low_head, k, zero), jnp.where(low_head, zero, kr),
                  jnp.where(low_head, kr, zero), jnp.where(low_head, zero, k))
    v_variants = (jnp.where(low_head, v, zero), jnp.where(low_head, zero, vr),
                  jnp.where(low_head, vr, zero), jnp.where(low_head, zero, v))
    for i in range(2 * N_KV_HEADS):
        k_scr[i, BLOCK:BLOCK + TQ, :] = k_variants[i].astype(_BF)
        v_scr[i, BLOCK:BLOCK + TQ, 0:LANES] = v_variants[i].astype(_BF)
    q_unit(0)
    za_unit(0)

    qi = lax.broadcasted_iota(jnp.int32, (BLOCK, 2 * BLOCK), 0)
    kj = lax.broadcasted_iota(jnp.int32, (BLOCK, 2 * BLOCK), 1)
    band = (kj > qi) & (kj <= qi + BLOCK)
    low_head_b = lax.broadcasted_iota(jnp.int32, (BLOCK, LANES), 1) < HEAD_DIM

    def softmax_parts(sc, mask, sink):
        sc = jnp.where(mask, sc, NEG)
        m = jnp.maximum(jnp.max(sc, axis=-1, keepdims=True), sink)
        return jnp.exp(sc - m).astype(_BF), jnp.exp(sink - m)

    dense = dense_units()
    n_iters = (N_PAIRS // PAIR_STACK) * (TQ // BLOCK)
    n_left = N_DENSE_UNITS
    it = 0
    for c0 in range(0, N_PAIRS, PAIR_STACK):
        kvh = c0 // PAIRS_PER_KV
        pairs = range(c0, c0 + PAIR_STACK)
        for b in range(TQ // BLOCK):
            r0 = b * BLOCK
            first_key = jnp.where(j == 0, BLOCK, 0) if b == 0 else 0
            mask = band & (kj >= first_key)
            qs = jnp.concatenate(
                [q_scr[r0:r0 + BLOCK, c * LANES:(c + 1) * LANES] for c in pairs], axis=0)
            ka = k_scr[2 * kvh, r0:r0 + 2 * BLOCK, :]
            kb = k_scr[2 * kvh + 1, r0:r0 + 2 * BLOCK, :]
            va = v_scr[2 * kvh, r0:r0 + 2 * BLOCK, :]
            vb = v_scr[2 * kvh + 1, r0:r0 + 2 * BLOCK, :]
            sa = _dot_nt(qs, ka)
            sb = _dot_nt(qs, kb)
            for _ in range(n_left // n_iters + (1 if it < n_left % n_iters else 0)):
                next(dense)
            it += 1
            parts_a = [softmax_parts(sa[i * BLOCK:(i + 1) * BLOCK], mask, sink_ref[PAIR * c])
                       for i, c in enumerate(pairs)]
            parts_b = [softmax_parts(sb[i * BLOCK:(i + 1) * BLOCK], mask, sink_ref[PAIR * c + 1])
                       for i, c in enumerate(pairs)]
            pa = jnp.concatenate([p for p, _ in parts_a], axis=0)
            pb = jnp.concatenate([p for p, _ in parts_b], axis=0)
            ols = _dot(pa, va) + _dot(pb, vb)
            for i, c in enumerate(pairs):
                ol = ols[i * BLOCK:(i + 1) * BLOCK]
                denom = ol[:, LANES:] + jnp.where(low_head_b, parts_a[i][1], parts_b[i][1])
                gate = sza_scr[r0:r0 + BLOCK, c * LANES:(c + 1) * LANES]
                ub_scr[r0:r0 + BLOCK, c * LANES:(c + 1) * LANES] = (
                    gate * (ol[:, :LANES] * (1.0 / denom))).astype(_BF)
    for _ in dense:
        pass

    for r0, r1 in zip(EPILOGUE_SPLIT[:-1], EPILOGUE_SPLIT[1:]):
        rows = slice(r0, r1)
        for n in range(D_MODEL // MXU_N):
            cols = slice(n * MXU_N, (n + 1) * MXU_N)
            yb = _dot(ub_scr[rows, :], wpa_ref[:, cols])
            ua_scr[rows, cols] = (t_scr[rows, cols] + sgb_scr[rows, cols] * yb).astype(_BF)
        y = _dot(ua_scr[rows, :], wout_ref[...])
        ms2 = jnp.mean(y * y, axis=-1, keepdims=True)
        o_ref[0, rows, :] = x_ref[0, rows, :] + y * lax.rsqrt(ms2 + RMS_EPS) * gpost_ref[...]


def _rope_tables(t):
    inv_freq = ROPE_THETA ** (-jnp.arange(0, HEAD_DIM, 2, dtype=_F32) / HEAD_DIM)
    inv_freq = jnp.tile(inv_freq, 2 * PAIR)
    half = jnp.ones((HEAD_DIM // 2,), _F32)
    sign = jnp.tile(jnp.concatenate([-half, half]), PAIR)
    ang = jnp.arange(t).astype(_F32)[:, None] * inv_freq[None, :]
    return jnp.stack([jnp.cos(ang), jnp.sin(ang) * sign])


def _resident(shape):
    return pl.BlockSpec(shape, lambda b, j: (0,) * len(shape), pipeline_mode=pl.Buffered(1))


def _hybrid_layer(x, g_pre, g_post, w_in, w_conv, sinks, w_proj_conv, w_proj_attn, w_out):
    bsz, t, d = x.shape
    assert d == D_MODEL and t % TQ == 0 and w_in.shape == (D_MODEL, D_IN)
    tables = _rope_tables(t)
    return pl.pallas_call(
        _layer_kernel,
        out_shape=jax.ShapeDtypeStruct(x.shape, x.dtype),
        grid=(bsz, t // TQ),
        in_specs=[
            pl.BlockSpec(memory_space=pltpu.SMEM),
            pl.BlockSpec((1, TQ, D_MODEL), lambda b, j: (b, j, 0)),
            _resident((1, D_MODEL)),
            _resident((1, D_MODEL)),
            _resident((CONV_WIDTH, D_CONV)),
            pl.BlockSpec((2, TQ, LANES), lambda b, j: (0, j, 0)),
            _resident((D_MODEL, D_IN)),
            _resident((D_CONV, D_MODEL)),
            _resident((D_ATTN, D_MODEL)),
            _resident((D_MODEL, D_MODEL)),
        ],
        out_specs=pl.BlockSpec((1, TQ, D_MODEL), lambda b, j: (b, j, 0)),
        scratch_shapes=[
            pltpu.VMEM((TQ, D_MODEL), _BF),
            pltpu.VMEM((TQ, D_CONV), _BF),
            pltpu.VMEM((TQ, D_ATTN), _BF),
            pltpu.VMEM((TQ, D_ATTN), _BF),
            pltpu.VMEM((TQ, D_ATTN), _BF),
            pltpu.VMEM((2 * N_KV_HEADS, BLOCK + TQ, LANES), _BF),
            pltpu.VMEM((2 * N_KV_HEADS, BLOCK + TQ, 2 * LANES), _BF),
            pltpu.VMEM((SUBLANES, D_CONV), _F32),
            pltpu.VMEM((TQ, D_MODEL), _F32),
            pltpu.VMEM((TQ, D_MODEL), _BF),
        ],
        compiler_params=pltpu.CompilerParams(
            dimension_semantics=("arbitrary", "arbitrary"),
            vmem_limit_bytes=VMEM_LIMIT,
        ),
        name="hybrid_layer",
    )(sinks.astype(_F32), x, g_pre.reshape(1, D_MODEL), g_post.reshape(1, D_MODEL), w_conv,
      tables, w_in.astype(_BF), w_proj_conv.astype(_BF), w_proj_attn.astype(_BF),
      w_out.astype(_BF))


def kernel(x, g_pre, g_post, w_in, w_conv, sinks, w_proj_conv, w_proj_attn, w_out):
    for layer in range(g_pre.shape[0]):
        x = _hybrid_layer(x, g_pre[layer], g_post[layer], w_in[layer], w_conv[layer],
                          sinks[layer], w_proj_conv[layer], w_proj_attn[layer], w_out[layer])
    return x
```

```python
import jax
import jax.numpy as jnp
from jax import lax
from jax.experimental import pallas as pl
from jax.experimental.pallas import tpu as pltpu

D_MODEL = 1024
D_CONV = D_MODEL
CONV_WIDTH = 3
HEAD_DIM = 64
N_HEADS = D_MODEL // HEAD_DIM
N_KV_HEADS = 2
GROUP = N_HEADS // N_KV_HEADS
D_ATTN = N_HEADS * HEAD_DIM
D_KV = N_KV_HEADS * HEAD_DIM
WINDOW = 128
BLOCK = WINDOW
ROPE_THETA = 10000.0
RMS_EPS = 1e-6

OFF_XC = 0
OFF_BG = OFF_XC + D_CONV
OFF_CG = OFF_BG + D_CONV
OFF_ZC = OFF_CG + D_CONV
OFF_Q = OFF_ZC + D_CONV
OFF_K = OFF_Q + D_ATTN
OFF_V = OFF_K + D_KV
OFF_ZA = OFF_V + D_KV
OFF_GA = OFF_ZA + D_ATTN
OFF_GB = OFF_GA + D_MODEL
D_IN = OFF_GB + D_MODEL

LANES = 128
SUBLANES = 8
MXU_N = 256
PAIR = LANES // HEAD_DIM
N_PAIRS = N_HEADS // PAIR
PAIRS_PER_KV = GROUP // PAIR
PAIR_STACK = 2
N_DENSE_UNITS = (2 * (D_ATTN // MXU_N - 1) + 4 * (D_CONV // MXU_N) + 2 * (D_MODEL // MXU_N)
                 + D_MODEL // MXU_N)
TQ = 1024
DOT_ROWS = 512
EPILOGUE_ROWS = 256
EPILOGUE_SPLIT = tuple(range(0, TQ + 1, EPILOGUE_ROWS))
NEG = -1e30
VMEM_LIMIT = 62 * 1024 * 1024

assert D_KV == LANES and PAIR == 2 and PAIRS_PER_KV % PAIR_STACK == 0 and D_MODEL % MXU_N == 0
assert TQ % BLOCK == 0 and TQ % DOT_ROWS == 0 and TQ % EPILOGUE_ROWS == 0

_BF = jnp.bfloat16
_F32 = jnp.float32


def _dot(a, b):
    m = a.shape[0]
    if m <= DOT_ROWS:
        return jnp.dot(a, b, preferred_element_type=_F32)
    return jnp.concatenate(
        [jnp.dot(a[r:r + DOT_ROWS], b, preferred_element_type=_F32) for r in range(0, m, DOT_ROWS)],
        axis=0)


def _dot_nt(a, b):
    return lax.dot_general(a, b, (((1,), (1,)), ((), ())), preferred_element_type=_F32)


def _sigmoid(z):
    return 1.0 / (1.0 + jnp.exp(-z))


def _layer_kernel(sink_ref, x_ref, gpre_ref, gpost_ref, wconv_ref, tab_ref,
                  win_ref, wpc_ref, wpa_ref, wout_ref, o_ref,
                  h_scr, ua_scr, ub_scr, q_scr, sza_scr, k_scr, v_scr, ucarry, t_scr, sgb_scr):
    j = pl.program_id(1)

    @pl.when((pl.program_id(0) == 0) & (j == 0))
    def _():
        low = lax.broadcasted_iota(jnp.int32, (BLOCK + TQ, LANES), 1) < HEAD_DIM
        ones_lo = jnp.where(low, 1.0, 0.0).astype(_BF)
        ones_hi = jnp.where(low, 0.0, 1.0).astype(_BF)
        for i in range(2 * N_KV_HEADS):
            v_scr[i, :, LANES:] = ones_hi if i % 2 else ones_lo

    @pl.when(j == 0)
    def _():
        ucarry[...] = jnp.zeros_like(ucarry)
        k_scr[:, 0:BLOCK, :] = jnp.zeros((2 * N_KV_HEADS, BLOCK, LANES), _BF)
        v_scr[:, 0:BLOCK, 0:LANES] = jnp.zeros((2 * N_KV_HEADS, BLOCK, LANES), _BF)

    @pl.when(j != 0)
    def _():
        k_scr[:, 0:BLOCK, :] = k_scr[:, TQ:TQ + BLOCK, :]
        v_scr[:, 0:BLOCK, 0:LANES] = v_scr[:, TQ:TQ + BLOCK, 0:LANES]

    x = x_ref[0]
    ms = jnp.mean(x * x, axis=-1, keepdims=True)
    h_scr[...] = (x * lax.rsqrt(ms + RMS_EPS) * gpre_ref[...]).astype(_BF)

    def proj(off, n):
        lo = off + n * MXU_N
        return _dot(h_scr[...], win_ref[:, lo:lo + MXU_N])

    lane = lax.broadcasted_iota(jnp.int32, (TQ, LANES), 1)
    first_half = (lane & (HEAD_DIM // 2)) == 0
    low_head = lane < HEAD_DIM

    def rope(z, cos, sin):
        rot = jnp.where(first_half, pltpu.roll(z, LANES - HEAD_DIM // 2, 1),
                        pltpu.roll(z, HEAD_DIM // 2, 1))
        return z * cos + rot * sin

    def q_unit(n):
        qf = proj(OFF_Q, n) * (HEAD_DIM ** -0.5)
        for c in range(MXU_N // LANES):
            qc = rope(qf[:, c * LANES:(c + 1) * LANES], tab_ref[0], tab_ref[1])
            lo = n * MXU_N + c * LANES
            q_scr[:, lo:lo + LANES] = qc.astype(_BF)

    def za_unit(n):
        za = proj(OFF_ZA, n)
        sza_scr[:, n * MXU_N:(n + 1) * MXU_N] = (za * _sigmoid(za)).astype(_BF)

    row = lax.broadcasted_iota(jnp.int32, (TQ, MXU_N), 0)

    def dense_units():
        for n in range(1, D_ATTN // MXU_N):
            q_unit(n)
            yield
            za_unit(n)
            yield
        for n in range(D_MODEL // MXU_N):
            sgb_scr[:, n * MXU_N:(n + 1) * MXU_N] = _sigmoid(proj(OFF_GB, n)).astype(_BF)
            yield
        for n in range(D_CONV // MXU_N):
            cols = slice(n * MXU_N, (n + 1) * MXU_N)
            xc = proj(OFF_XC, n)
            yield
            u = proj(OFF_CG, n) * xc
            prev = ucarry[:, cols]
            p1 = prev[SUBLANES - 1:SUBLANES, :]
            p2 = prev[SUBLANES - 2:SUBLANES - 1, :]
            u1 = jnp.where(row == 0, p1, pltpu.roll(u, 1, 0))
            u2 = jnp.where(row == 0, p2, jnp.where(row == 1, p1, pltpu.roll(u, 2, 0)))
            ucarry[:, cols] = u[TQ - SUBLANES:, :]
            w = wconv_ref[:, cols]
            y = w[0:1, :] * u2 + w[1:2, :] * u1 + w[2:3, :] * u
            yield
            y = proj(OFF_BG, n) * y
            yield
            zc = proj(OFF_ZC, n)
            ua_scr[:, cols] = ((zc * _sigmoid(zc)) * y).astype(_BF)
            yield
        for n in range(D_MODEL // MXU_N):
            cols = slice(n * MXU_N, (n + 1) * MXU_N)
            ya = _dot(ua_scr[...], wpc_ref[:, cols])
            yield
            t_scr[:, cols] = _sigmoid(proj(OFF_GA, n)) * ya
            yield

    kv = proj(OFF_K, 0)
    k = rope(kv[:, :D_KV], tab_ref[0], tab_ref[1])
    v = kv[:, D_KV:]
    kr = pltpu.roll(k, HEAD_DIM, 1)
    vr = pltpu.roll(v, HEAD_DIM, 1)
    zero = jnp.zeros_like(k)
    k_variants = (jnp.where(low_head, k, zero), jnp.where(low_head, zero, kr),
                  jnp.where(low_head, kr, zero), jnp.where(low_head, zero, k))
    v_variants = (jnp.where(low_head, v, zero), jnp.where(low_head, zero, vr),
                  jnp.where(low_head, vr, zero), jnp.where(low_head, zero, v))
    for i in range(2 * N_KV_HEADS):
        k_scr[i, BLOCK:BLOCK + TQ, :] = k_variants[i].astype(_BF)
        v_scr[i, BLOCK:BLOCK + TQ, 0:LANES] = v_variants[i].astype(_BF)
    q_unit(0)
    za_unit(0)

    qi = lax.broadcasted_iota(jnp.int32, (BLOCK, 2 * BLOCK), 0)
    kj = lax.broadcasted_iota(jnp.int32, (BLOCK, 2 * BLOCK), 1)
    band = (kj > qi) & (kj <= qi + BLOCK)
    low_head_b = lax.broadcasted_iota(jnp.int32, (BLOCK, LANES), 1) < HEAD_DIM

    def softmax_parts(sc, mask, sink):
        sc = jnp.where(mask, sc, NEG)
        m = jnp.maximum(jnp.max(sc, axis=-1, keepdims=True), sink)
        return jnp.exp(sc - m).astype(_BF), jnp.exp(sink - m)

    dense = dense_units()
    n_iters = (N_PAIRS // PAIR_STACK) * (TQ // BLOCK)
    n_left = N_DENSE_UNITS
    it = 0
    for c0 in range(0, N_PAIRS, PAIR_STACK):
        kvh = c0 // PAIRS_PER_KV
        pairs = range(c0, c0 + PAIR_STACK)
        for b in range(TQ // BLOCK):
            r0 = b * BLOCK
            first_key = jnp.where(j == 0, BLOCK, 0) if b == 0 else 0
            mask = band & (kj >= first_key)
            qs = jnp.concatenate(
                [q_scr[r0:r0 + BLOCK, c * LANES:(c + 1) * LANES] for c in pairs], axis=0)
            ka = k_scr[2 * kvh, r0:r0 + 2 * BLOCK, :]
            kb = k_scr[2 * kvh + 1, r0:r0 + 2 * BLOCK, :]
            va = v_scr[2 * kvh, r0:r0 + 2 * BLOCK, :]
            vb = v_scr[2 * kvh + 1, r0:r0 + 2 * BLOCK, :]
            sa = _dot_nt(qs, ka)
            sb = _dot_nt(qs, kb)
            for _ in range(n_left * (it + 1) // n_iters - n_left * it // n_iters):
                next(dense)
            it += 1
            parts_a = [softmax_parts(sa[i * BLOCK:(i + 1) * BLOCK], mask, sink_ref[PAIR * c])
                       for i, c in enumerate(pairs)]
            parts_b = [softmax_parts(sb[i * BLOCK:(i + 1) * BLOCK], mask, sink_ref[PAIR * c + 1])
                       for i, c in enumerate(pairs)]
            pa = jnp.concatenate([p for p, _ in parts_a], axis=0)
            pb = jnp.concatenate([p for p, _ in parts_b], axis=0)
            ols = _dot(pa, va) + _dot(pb, vb)
            for i, c in enumerate(pairs):
                ol = ols[i * BLOCK:(i + 1) * BLOCK]
                denom = ol[:, LANES:] + jnp.where(low_head_b, parts_a[i][1], parts_b[i][1])
                gate = sza_scr[r0:r0 + BLOCK, c * LANES:(c + 1) * LANES]
                ub_scr[r0:r0 + BLOCK, c * LANES:(c + 1) * LANES] = (
                    gate * (ol[:, :LANES] * (1.0 / denom))).astype(_BF)
    for _ in dense:
        pass

    for r0, r1 in zip(EPILOGUE_SPLIT[:-1], EPILOGUE_SPLIT[1:]):
        rows = slice(r0, r1)
        for n in range(D_MODEL // MXU_N):
            cols = slice(n * MXU_N, (n + 1) * MXU_N)
            yb = _dot(ub_scr[rows, :], wpa_ref[:, cols])
            ua_scr[rows, cols] = (t_scr[rows, cols] + sgb_scr[rows, cols] * yb).astype(_BF)
        y = _dot(ua_scr[rows, :], wout_ref[...])
        ms2 = jnp.mean(y * y, axis=-1, keepdims=True)
        o_ref[0, rows, :] = x_ref[0, rows, :] + y * lax.rsqrt(ms2 + RMS_EPS) * gpost_ref[...]


def _rope_tables(t):
    inv_freq = ROPE_THETA ** (-jnp.arange(0, HEAD_DIM, 2, dtype=_F32) / HEAD_DIM)
    inv_freq = jnp.tile(inv_freq, 2 * PAIR)
    half = jnp.ones((HEAD_DIM // 2,), _F32)
    sign = jnp.tile(jnp.concatenate([-half, half]), PAIR)
    ang = jnp.arange(t).astype(_F32)[:, None] * inv_freq[None, :]
    return jnp.stack([jnp.cos(ang), jnp.sin(ang) * sign])


def _resident(shape):
    return pl.BlockSpec(shape, lambda b, j: (0,) * len(shape), pipeline_mode=pl.Buffered(1))


def _hybrid_layer(x, g_pre, g_post, w_in, w_conv, sinks, w_proj_conv, w_proj_attn, w_out):
    bsz, t, d = x.shape
    assert d == D_MODEL and t % TQ == 0 and w_in.shape == (D_MODEL, D_IN)
    tables = _rope_tables(t)
    return pl.pallas_call(
        _layer_kernel,
        out_shape=jax.ShapeDtypeStruct(x.shape, x.dtype),
        grid=(bsz, t // TQ),
        in_specs=[
            pl.BlockSpec(memory_space=pltpu.SMEM),
            pl.BlockSpec((1, TQ, D_MODEL), lambda b, j: (b, j, 0)),
            _resident((1, D_MODEL)),
            _resident((1, D_MODEL)),
            _resident((CONV_WIDTH, D_CONV)),
            pl.BlockSpec((2, TQ, LANES), lambda b, j: (0, j, 0)),
            _resident((D_MODEL, D_IN)),
            _resident((D_CONV, D_MODEL)),
            _resident((D_ATTN, D_MODEL)),
            _resident((D_MODEL, D_MODEL)),
        ],
        out_specs=pl.BlockSpec((1, TQ, D_MODEL), lambda b, j: (b, j, 0)),
        scratch_shapes=[
            pltpu.VMEM((TQ, D_MODEL), _BF),
            pltpu.VMEM((TQ, D_CONV), _BF),
            pltpu.VMEM((TQ, D_ATTN), _BF),
            pltpu.VMEM((TQ, D_ATTN), _BF),
            pltpu.VMEM((TQ, D_ATTN), _BF),
            pltpu.VMEM((2 * N_KV_HEADS, BLOCK + TQ, LANES), _BF),
            pltpu.VMEM((2 * N_KV_HEADS, BLOCK + TQ, 2 * LANES), _BF),
            pltpu.VMEM((SUBLANES, D_CONV), _F32),
            pltpu.VMEM((TQ, D_MODEL), _F32),
            pltpu.VMEM((TQ, D_MODEL), _BF),
        ],
        compiler_params=pltpu.CompilerParams(
            dimension_semantics=("arbitrary", "arbitrary"),
            vmem_limit_bytes=VMEM_LIMIT,
        ),
        name="hybrid_layer",
    )(sinks.astype(_F32), x, g_pre.reshape(1, D_MODEL), g_post.reshape(1, D_MODEL), w_conv,
      tables, w_in.astype(_BF), w_proj_conv.astype(_BF), w_proj_attn.astype(_BF),
      w_out.astype(_BF))


def kernel(x, g_pre, g_post, w_in, w_conv, sinks, w_proj_conv, w_proj_attn, w_out):
    for layer in range(g_pre.shape[0]):
        x = _hybrid_layer(x, g_pre[layer], g_post[layer], w_in[layer], w_conv[layer],
                          sinks[layer], w_proj_conv[layer], w_proj_attn[layer], w_out[layer])
    return x
```

```python
import jax
import jax.numpy as jnp
from jax import lax
from jax.experimental import pallas as pl
from jax.experimental.pallas import tpu as pltpu

D_MODEL = 1024
D_CONV = D_MODEL
CONV_WIDTH = 3
HEAD_DIM = 64
N_HEADS = D_MODEL // HEAD_DIM
N_KV_HEADS = 2
GROUP = N_HEADS // N_KV_HEADS
D_ATTN = N_HEADS * HEAD_DIM
D_KV = N_KV_HEADS * HEAD_DIM
WINDOW = 128
BLOCK = WINDOW
ROPE_THETA = 10000.0
RMS_EPS = 1e-6

OFF_XC = 0
OFF_BG = OFF_XC + D_CONV
OFF_CG = OFF_BG + D_CONV
OFF_ZC = OFF_CG + D_CONV
OFF_Q = OFF_ZC + D_CONV
OFF_K = OFF_Q + D_ATTN
OFF_V = OFF_K + D_KV
OFF_ZA = OFF_V + D_KV
OFF_GA = OFF_ZA + D_ATTN
OFF_GB = OFF_GA + D_MODEL
D_IN = OFF_GB + D_MODEL

LANES = 128
SUBLANES = 8
MXU_N = 256
PAIR = LANES // HEAD_DIM
N_PAIRS = N_HEADS // PAIR
PAIRS_PER_KV = GROUP // PAIR
PAIR_STACK = 2
N_DENSE_UNITS = (2 * (D_ATTN // MXU_N - 1) + 4 * (D_CONV // MXU_N) + 2 * (D_MODEL // MXU_N)
                 + D_MODEL // MXU_N)
TQ = 1024
DOT_ROWS = 512
EPILOGUE_ROWS = 256
EPILOGUE_SPLIT = tuple(range(0, TQ + 1, EPILOGUE_ROWS))
NEG = -1e30
VMEM_LIMIT = 62 * 1024 * 1024

assert D_KV == LANES and PAIR == 2 and PAIRS_PER_KV % PAIR_STACK == 0 and D_MODEL % MXU_N == 0
assert TQ % BLOCK == 0 and TQ % DOT_ROWS == 0 and TQ % EPILOGUE_ROWS == 0

_BF = jnp.bfloat16
_F32 = jnp.float32


def _dot(a, b):
    m = a.shape[0]
    if m <= DOT_ROWS:
        return jnp.dot(a, b, preferred_element_type=_F32)
    return jnp.concatenate(
        [jnp.dot(a[r:r + DOT_ROWS], b, preferred_element_type=_F32) for r in range(0, m, DOT_ROWS)],
        axis=0)


def _dot_nt(a, b):
    return lax.dot_general(a, b, (((1,), (1,)), ((), ())), preferred_element_type=_F32)


def _sigmoid(z):
    return 1.0 / (1.0 + jnp.exp(-z))


def _layer_kernel(sink_ref, x_ref, gpre_ref, gpost_ref, wconv_ref, tab_ref,
                  win_ref, wpc_ref, wpa_ref, wout_ref, o_ref,
                  h_scr, ua_scr, ub_scr, q_scr, sza_scr, k_scr, v_scr, ucarry, t_scr, sgb_scr):
    j = pl.program_id(1)

    @pl.when((pl.program_id(0) == 0) & (j == 0))
    def _():
        low = lax.broadcasted_iota(jnp.int32, (BLOCK + TQ, LANES), 1) < HEAD_DIM
        ones_lo = jnp.where(low, 1.0, 0.0).astype(_BF)
        ones_hi = jnp.where(low, 0.0, 1.0).astype(_BF)
        for i in range(2 * N_KV_HEADS):
            v_scr[i, :, LANES:] = ones_hi if i % 2 else ones_lo

    @pl.when(j == 0)
    def _():
        ucarry[...] = jnp.zeros_like(ucarry)
        k_scr[:, 0:BLOCK, :] = jnp.zeros((2 * N_KV_HEADS, BLOCK, LANES), _BF)
        v_scr[:, 0:BLOCK, 0:LANES] = jnp.zeros((2 * N_KV_HEADS, BLOCK, LANES), _BF)

    @pl.when(j != 0)
    def _():
        k_scr[:, 0:BLOCK, :] = k_scr[:, TQ:TQ + BLOCK, :]
        v_scr[:, 0:BLOCK, 0:LANES] = v_scr[:, TQ:TQ + BLOCK, 0:LANES]

    x = x_ref[0]
    ms = jnp.mean(x * x, axis=-1, keepdims=True)
    h_scr[...] = (x * lax.rsqrt(ms + RMS_EPS) * gpre_ref[...]).astype(_BF)

    def proj(off, n):
        lo = off + n * MXU_N
        return _dot(h_scr[...], win_ref[:, lo:lo + MXU_N])

    lane = lax.broadcasted_iota(jnp.int32, (TQ, LANES), 1)
    first_half = (lane & (HEAD_DIM // 2)) == 0
    low_head = lane < HEAD_DIM

    def rope(z, cos, sin):
        rot = jnp.where(first_half, pltpu.roll(z, LANES - HEAD_DIM // 2, 1),
                        pltpu.roll(z, HEAD_DIM // 2, 1))
        return z * cos + rot * sin

    def q_unit(n):
        qf = proj(OFF_Q, n) * (HEAD_DIM ** -0.5)
        for c in range(MXU_N // LANES):
            qc = rope(qf[:, c * LANES:(c + 1) * LANES], tab_ref[0], tab_ref[1])
            lo = n * MXU_N + c * LANES
            q_scr[:, lo:lo + LANES] = qc.astype(_BF)

    def za_unit(n):
        za = proj(OFF_ZA, n)
        sza_scr[:, n * MXU_N:(n + 1) * MXU_N] = (za * _sigmoid(za)).astype(_BF)

    row = lax.broadcasted_iota(jnp.int32, (TQ, MXU_N), 0)

    def dense_units():
        for n in range(1, D_ATTN // MXU_N):
            q_unit(n)
            yield
            za_unit(n)
            yield
        for n in range(D_CONV // MXU_N):
            cols = slice(n * MXU_N, (n + 1) * MXU_N)
            xc = proj(OFF_XC, n)
            yield
            u = proj(OFF_CG, n) * xc
            prev = ucarry[:, cols]
            p1 = prev[SUBLANES - 1:SUBLANES, :]
            p2 = prev[SUBLANES - 2:SUBLANES - 1, :]
            u1 = jnp.where(row == 0, p1, pltpu.roll(u, 1, 0))
            u2 = jnp.where(row == 0, p2, jnp.where(row == 1, p1, pltpu.roll(u, 2, 0)))
            ucarry[:, cols] = u[TQ - SUBLANES:, :]
            w = wconv_ref[:, cols]
            y = w[0:1, :] * u2 + w[1:2, :] * u1 + w[2:3, :] * u
            yield
            y = proj(OFF_BG, n) * y
            yield
            zc = proj(OFF_ZC, n)
            ua_scr[:, cols] = ((zc * _sigmoid(zc)) * y).astype(_BF)
            yield
        for n in range(D_MODEL // MXU_N):
            cols = slice(n * MXU_N, (n + 1) * MXU_N)
            ya = _dot(ua_scr[...], wpc_ref[:, cols])
            yield
            t_scr[:, cols] = _sigmoid(proj(OFF_GA, n)) * ya
            yield
        for n in range(D_MODEL // MXU_N):
            sgb_scr[:, n * MXU_N:(n + 1) * MXU_N] = _sigmoid(proj(OFF_GB, n)).astype(_BF)
            yield

    kv = proj(OFF_K, 0)
    k = rope(kv[:, :D_KV], tab_ref[0], tab_ref[1])
    v = kv[:, D_KV:]
    kr = pltpu.roll(k, HEAD_DIM, 1)
    vr = pltpu.roll(v, HEAD_DIM, 1)
    zero = jnp.zeros_like(k)
    k_variants = (jnp.where(low_head, k, zero), jnp.where(low_head, zero, kr),
                  jnp.where(low_head, kr, zero), jnp.where(low_head, zero, k))
    v_variants = (jnp.where(low_head, v, zero), jnp.where(low_head, zero, vr),
                  jnp.where(low_head, vr, zero), jnp.where(low_head, zero, v))
    for i in range(2 * N_KV_HEADS):
        k_scr[i, BLOCK:BLOCK + TQ, :] = k_variants[i].astype(_BF)
        v_scr[i, BLOCK:BLOCK + TQ, 0:LANES] = v_variants[i].astype(_BF)
    q_unit(0)
    za_unit(0)

    qi = lax.broadcasted_iota(jnp.int32, (BLOCK, 2 * BLOCK), 0)
    kj = lax.broadcasted_iota(jnp.int32, (BLOCK, 2 * BLOCK), 1)
    band = (kj > qi) & (kj <= qi + BLOCK)
    low_head_b = lax.broadcasted_iota(jnp.int32, (BLOCK, LANES), 1) < HEAD_DIM

    def softmax_parts(sc, mask, sink):
        sc = jnp.where(mask, sc, NEG)
        m = jnp.maximum(jnp.max(sc, axis=-1, keepdims=True), sink)
        return jnp.exp(sc - m).astype(_BF), jnp.exp(sink - m)

    dense = dense_units()
    n_iters = (N_PAIRS // PAIR_STACK) * (TQ // BLOCK)
    n_left = N_DENSE_UNITS
    it = 0
    for c0 in range(0, N_PAIRS, PAIR_STACK):
        kvh = c0 // PAIRS_PER_KV
        pairs = range(c0, c0 + PAIR_STACK)
        for b in range(TQ // BLOCK):
            r0 = b * BLOCK
            first_key = jnp.where(j == 0, BLOCK, 0) if b == 0 else 0
            mask = band & (kj >= first_key)
            qs = jnp.concatenate(
                [q_scr[r0:r0 + BLOCK, c * LANES:(c + 1) * LANES] for c in pairs], axis=0)
            ka = k_scr[2 * kvh, r0:r0 + 2 * BLOCK, :]
            kb = k_scr[2 * kvh + 1, r0:r0 + 2 * BLOCK, :]
            va = v_scr[2 * kvh, r0:r0 + 2 * BLOCK, :]
            vb = v_scr[2 * kvh + 1, r0:r0 + 2 * BLOCK, :]
            sa = _dot_nt(qs, ka)
            sb = _dot_nt(qs, kb)
            for _ in range(n_left // n_iters + (1 if it >= n_iters - n_left % n_iters else 0)):
                next(dense)
            it += 1
            parts_a = [softmax_parts(sa[i * BLOCK:(i + 1) * BLOCK], mask, sink_ref[PAIR * c])
                       for i, c in enumerate(pairs)]
            parts_b = [softmax_parts(sb[i * BLOCK:(i + 1) * BLOCK], mask, sink_ref[PAIR * c + 1])
                       for i, c in enumerate(pairs)]
            pa = jnp.concatenate([p for p, _ in parts_a], axis=0)
            pb = jnp.concatenate([p for p, _ in parts_b], axis=0)
            ols = _dot(pa, va) + _dot(pb, vb)
            for i, c in enumerate(pairs):
                ol = ols[i * BLOCK:(i + 1) * BLOCK]
                denom = ol[:, LANES:] + jnp.where(low_head_b, parts_a[i][1], parts_b[i][1])
                gate = sza_scr[r0:r0 + BLOCK, c * LANES:(c + 1) * LANES]
                ub_scr[r0:r0 + BLOCK, c * LANES:(c + 1) * LANES] = (
                    gate * (ol[:, :LANES] * (1.0 / denom))).astype(_BF)
    for _ in dense:
        pass

    for r0, r1 in zip(EPILOGUE_SPLIT[:-1], EPILOGUE_SPLIT[1:]):
        rows = slice(r0, r1)
        for n in range(D_MODEL // MXU_N):
            cols = slice(n * MXU_N, (n + 1) * MXU_N)
            yb = _dot(ub_scr[rows, :], wpa_ref[:, cols])
            ua_scr[rows, cols] = (t_scr[rows, cols] + sgb_scr[rows, cols] * yb).astype(_BF)
        y = _dot(ua_scr[rows, :], wout_ref[...])
        ms2 = jnp.mean(y * y, axis=-1, keepdims=True)
        o_ref[0, rows, :] = x_ref[0, rows, :] + y * lax.rsqrt(ms2 + RMS_EPS) * gpost_ref[...]


def _rope_tables(t):
    inv_freq = ROPE_THETA ** (-jnp.arange(0, HEAD_DIM, 2, dtype=_F32) / HEAD_DIM)
    inv_freq = jnp.tile(inv_freq, 2 * PAIR)
    half = jnp.ones((HEAD_DIM // 2,), _F32)
    sign = jnp.tile(jnp.concatenate([-half, half]), PAIR)
    ang = jnp.arange(t).astype(_F32)[:, None] * inv_freq[None, :]
    return jnp.stack([jnp.cos(ang), jnp.sin(ang) * sign])


def _resident(shape):
    return pl.BlockSpec(shape, lambda b, j: (0,) * len(shape), pipeline_mode=pl.Buffered(1))


def _hybrid_layer(x, g_pre, g_post, w_in, w_conv, sinks, w_proj_conv, w_proj_attn, w_out):
    bsz, t, d = x.shape
    assert d == D_MODEL and t % TQ == 0 and w_in.shape == (D_MODEL, D_IN)
    tables = _rope_tables(t)
    return pl.pallas_call(
        _layer_kernel,
        out_shape=jax.ShapeDtypeStruct(x.shape, x.dtype),
        grid=(bsz, t // TQ),
        in_specs=[
            pl.BlockSpec(memory_space=pltpu.SMEM),
            pl.BlockSpec((1, TQ, D_MODEL), lambda b, j: (b, j, 0)),
            _resident((1, D_MODEL)),
            _resident((1, D_MODEL)),
            _resident((CONV_WIDTH, D_CONV)),
            pl.BlockSpec((2, TQ, LANES), lambda b, j: (0, j, 0)),
            _resident((D_MODEL, D_IN)),
            _resident((D_CONV, D_MODEL)),
            _resident((D_ATTN, D_MODEL)),
            _resident((D_MODEL, D_MODEL)),
        ],
        out_specs=pl.BlockSpec((1, TQ, D_MODEL), lambda b, j: (b, j, 0)),
        scratch_shapes=[
            pltpu.VMEM((TQ, D_MODEL), _BF),
            pltpu.VMEM((TQ, D_CONV), _BF),
            pltpu.VMEM((TQ, D_ATTN), _BF),
            pltpu.VMEM((TQ, D_ATTN), _BF),
            pltpu.VMEM((TQ, D_ATTN), _BF),
            pltpu.VMEM((2 * N_KV_HEADS, BLOCK + TQ, LANES), _BF),
            pltpu.VMEM((2 * N_KV_HEADS, BLOCK + TQ, 2 * LANES), _BF),
            pltpu.VMEM((SUBLANES, D_CONV), _F32),
            pltpu.VMEM((TQ, D_MODEL), _F32),
            pltpu.VMEM((TQ, D_MODEL), _BF),
        ],
        compiler_params=pltpu.CompilerParams(
            dimension_semantics=("arbitrary", "arbitrary"),
            vmem_limit_bytes=VMEM_LIMIT,
        ),
        name="hybrid_layer",
    )(sinks.astype(_F32), x, g_pre.reshape(1, D_MODEL), g_post.reshape(1, D_MODEL), w_conv,
      tables, w_in.astype(_BF), w_proj_conv.astype(_BF), w_proj_attn.astype(_BF),
      w_out.astype(_BF))


def kernel(x, g_pre, g_post, w_in, w_conv, sinks, w_proj_conv, w_proj_attn, w_out):
    for layer in range(g_pre.shape[0]):
        x = _hybrid_layer(x, g_pre[layer], g_post[layer], w_in[layer], w_conv[layer],
                          sinks[layer], w_proj_conv[layer], w_proj_attn[layer], w_out[layer])
    return x
```

```python
import jax
import jax.numpy as jnp
from jax import lax
from jax.experimental import pallas as pl
from jax.experimental.pallas import tpu as pltpu

D_MODEL = 1024
D_CONV = D_MODEL
CONV_WIDTH = 3
HEAD_DIM = 64
N_HEADS = D_MODEL // HEAD_DIM
N_KV_HEADS = 2
GROUP = N_HEADS // N_KV_HEADS
D_ATTN = N_HEADS * HEAD_DIM
D_KV = N_KV_HEADS * HEAD_DIM
WINDOW = 128
BLOCK = WINDOW
ROPE_THETA = 10000.0
RMS_EPS = 1e-6

OFF_XC = 0
OFF_BG = OFF_XC + D_CONV
OFF_CG = OFF_BG + D_CONV
OFF_ZC = OFF_CG + D_CONV
OFF_Q = OFF_ZC + D_CONV
OFF_K = OFF_Q + D_ATTN
OFF_V = OFF_K + D_KV
OFF_ZA = OFF_V + D_KV
OFF_GA = OFF_ZA + D_ATTN
OFF_GB = OFF_GA + D_MODEL
D_IN = OFF_GB + D_MODEL

LANES = 128
SUBLANES = 8
MXU_N = 256
PAIR = LANES // HEAD_DIM
N_PAIRS = N_HEADS // PAIR
PAIRS_PER_KV = GROUP // PAIR
PAIR_STACK = 2
N_DENSE_UNITS = (2 * (D_ATTN // MXU_N - 1) + 4 * (D_CONV // MXU_N) + 2 * (D_MODEL // MXU_N)
                 + D_MODEL // MXU_N)
TQ = 1024
DOT_ROWS = 512
EPILOGUE_ROWS = 256
EPILOGUE_SPLIT = tuple(range(0, TQ + 1, EPILOGUE_ROWS))
NEG = -1e30
VMEM_LIMIT = 62 * 1024 * 1024

assert D_KV == LANES and PAIR == 2 and PAIRS_PER_KV % PAIR_STACK == 0 and D_MODEL % MXU_N == 0
assert TQ % BLOCK == 0 and TQ % DOT_ROWS == 0 and TQ % EPILOGUE_ROWS == 0

_BF = jnp.bfloat16
_F32 = jnp.float32


def _dot(a, b):
    m = a.shape[0]
    if m <= DOT_ROWS:
        return jnp.dot(a, b, preferred_element_type=_F32)
    return jnp.concatenate(
        [jnp.dot(a[r:r + DOT_ROWS], b, preferred_element_type=_F32) for r in range(0, m, DOT_ROWS)],
        axis=0)


def _dot_nt(a, b):
    return lax.dot_general(a, b, (((1,), (1,)), ((), ())), preferred_element_type=_F32)


def _sigmoid(z):
    return 1.0 / (1.0 + jnp.exp(-z))


def _layer_kernel(sink_ref, x_ref, gpre_ref, gpost_ref, wconv_ref, tab_ref,
                  win_ref, wpc_ref, wpa_ref, wout_ref, o_ref,
                  h_scr, ua_scr, ub_scr, q_scr, sza_scr, k_scr, v_scr, ucarry, t_scr, sgb_scr):
    j = pl.program_id(1)

    @pl.when((pl.program_id(0) == 0) & (j == 0))
    def _():
        low = lax.broadcasted_iota(jnp.int32, (BLOCK + TQ, LANES), 1) < HEAD_DIM
        ones_lo = jnp.where(low, 1.0, 0.0).astype(_BF)
        ones_hi = jnp.where(low, 0.0, 1.0).astype(_BF)
        for i in range(2 * N_KV_HEADS):
            v_scr[i, :, LANES:] = ones_hi if i % 2 else ones_lo

    @pl.when(j == 0)
    def _():
        ucarry[...] = jnp.zeros_like(ucarry)
        k_scr[:, 0:BLOCK, :] = jnp.zeros((2 * N_KV_HEADS, BLOCK, LANES), _BF)
        v_scr[:, 0:BLOCK, 0:LANES] = jnp.zeros((2 * N_KV_HEADS, BLOCK, LANES), _BF)

    @pl.when(j != 0)
    def _():
        k_scr[:, 0:BLOCK, :] = k_scr[:, TQ:TQ + BLOCK, :]
        v_scr[:, 0:BLOCK, 0:LANES] = v_scr[:, TQ:TQ + BLOCK, 0:LANES]

    x = x_ref[0]
    ms = jnp.mean(x * x, axis=-1, keepdims=True)
    h_scr[...] = (x * lax.rsqrt(ms + RMS_EPS) * gpre_ref[...]).astype(_BF)

    def proj(off, n):
        lo = off + n * MXU_N
        return _dot(h_scr[...], win_ref[:, lo:lo + MXU_N])

    lane = lax.broadcasted_iota(jnp.int32, (TQ, LANES), 1)
    first_half = (lane & (HEAD_DIM // 2)) == 0
    low_head = lane < HEAD_DIM

    def rope(z, cos, sin):
        rot = jnp.where(first_half, pltpu.roll(z, LANES - HEAD_DIM // 2, 1),
                        pltpu.roll(z, HEAD_DIM // 2, 1))
        return z * cos + rot * sin

    def q_unit(n):
        qf = proj(OFF_Q, n) * (HEAD_DIM ** -0.5)
        for c in range(MXU_N // LANES):
            qc = rope(qf[:, c * LANES:(c + 1) * LANES], tab_ref[0], tab_ref[1])
            lo = n * MXU_N + c * LANES
            q_scr[:, lo:lo + LANES] = qc.astype(_BF)

    def za_unit(n):
        za = proj(OFF_ZA, n)
        sza_scr[:, n * MXU_N:(n + 1) * MXU_N] = (za * _sigmoid(za)).astype(_BF)

    row = lax.broadcasted_iota(jnp.int32, (TQ, MXU_N), 0)

    def dense_units():
        for n in range(1, D_ATTN // MXU_N):
            q_unit(n)
            yield
            za_unit(n)
            yield
        for n in range(D_CONV // MXU_N):
            cols = slice(n * MXU_N, (n + 1) * MXU_N)
            xc = proj(OFF_XC, n)
            yield
            u = proj(OFF_CG, n) * xc
            prev = ucarry[:, cols]
            p1 = prev[SUBLANES - 1:SUBLANES, :]
            p2 = prev[SUBLANES - 2:SUBLANES - 1, :]
            u1 = jnp.where(row == 0, p1, pltpu.roll(u, 1, 0))
            u2 = jnp.where(row == 0, p2, jnp.where(row == 1, p1, pltpu.roll(u, 2, 0)))
            ucarry[:, cols] = u[TQ - SUBLANES:, :]
            w = wconv_ref[:, cols]
            y = w[0:1, :] * u2 + w[1:2, :] * u1 + w[2:3, :] * u
            yield
            y = proj(OFF_BG, n) * y
            yield
            zc = proj(OFF_ZC, n)
            ua_scr[:, cols] = ((zc * _sigmoid(zc)) * y).astype(_BF)
            yield
        for n in range(D_MODEL // MXU_N):
            cols = slice(n * MXU_N, (n + 1) * MXU_N)
            ya = _dot(ua_scr[...], wpc_ref[:, cols])
            yield
            t_scr[:, cols] = _sigmoid(proj(OFF_GA, n)) * ya
            yield
        for n in range(D_MODEL // MXU_N):
            sgb_scr[:, n * MXU_N:(n + 1) * MXU_N] = _sigmoid(proj(OFF_GB, n)).astype(_BF)
            yield

    kv = proj(OFF_K, 0)
    k = rope(kv[:, :D_KV], tab_ref[0], tab_ref[1])
    v = kv[:, D_KV:]
    kr = pltpu.roll(k, HEAD_DIM, 1)
    vr = pltpu.roll(v, HEAD_DIM, 1)
    zero = jnp.zeros_like(k)
    k_variants = (jnp.where(low_head, k, zero), jnp.where(low_head, zero, kr),
                  jnp.where(low_head, kr, zero), jnp.where(low_head, zero, k))
    v_variants = (jnp.where(low_head, v, zero), jnp.where(low_head, zero, vr),
                  jnp.where(low_head, vr, zero), jnp.where(low_head, zero, v))
    for i in range(2 * N_KV_HEADS):
        k_scr[i, BLOCK:BLOCK + TQ, :] = k_variants[i].astype(_BF)
        v_scr[i, BLOCK:BLOCK + TQ, 0:LANES] = v_variants[i].astype(_BF)
    q_unit(0)
    za_unit(0)

    qi = lax.broadcasted_iota(jnp.int32, (BLOCK, 2 * BLOCK), 0)
    kj = lax.broadcasted_iota(jnp.int32, (BLOCK, 2 * BLOCK), 1)
    band = (kj > qi) & (kj <= qi + BLOCK)
    low_head_b = lax.broadcasted_iota(jnp.int32, (BLOCK, LANES), 1) < HEAD_DIM

    def softmax_parts(sc, mask, sink):
        sc = jnp.where(mask, sc, NEG)
        m = jnp.maximum(jnp.max(sc, axis=-1, keepdims=True), sink)
        return jnp.exp(sc - m).astype(_BF), jnp.exp(sink - m)

    dense = dense_units()
    n_iters = (N_PAIRS // PAIR_STACK) * (TQ // BLOCK)
    n_left = N_DENSE_UNITS
    it = 0
    stacks = list(range(0, N_PAIRS, PAIR_STACK))
    half = len(stacks) // 2
    for c0 in [s for pair in zip(stacks[:half], stacks[half:]) for s in pair]:
        kvh = c0 // PAIRS_PER_KV
        pairs = range(c0, c0 + PAIR_STACK)
        for b in range(TQ // BLOCK):
            r0 = b * BLOCK
            first_key = jnp.where(j == 0, BLOCK, 0) if b == 0 else 0
            mask = band & (kj >= first_key)
            qs = jnp.concatenate(
                [q_scr[r0:r0 + BLOCK, c * LANES:(c + 1) * LANES] for c in pairs], axis=0)
            ka = k_scr[2 * kvh, r0:r0 + 2 * BLOCK, :]
            kb = k_scr[2 * kvh + 1, r0:r0 + 2 * BLOCK, :]
            va = v_scr[2 * kvh, r0:r0 + 2 * BLOCK, :]
            vb = v_scr[2 * kvh + 1, r0:r0 + 2 * BLOCK, :]
            sa = _dot_nt(qs, ka)
            sb = _dot_nt(qs, kb)
            for _ in range(n_left * (it + 1) // n_iters - n_left * it // n_iters):
                next(dense)
            it += 1
            parts_a = [softmax_parts(sa[i * BLOCK:(i + 1) * BLOCK], mask, sink_ref[PAIR * c])
                       for i, c in enumerate(pairs)]
            parts_b = [softmax_parts(sb[i * BLOCK:(i + 1) * BLOCK], mask, sink_ref[PAIR * c + 1])
                       for i, c in enumerate(pairs)]
            pa = jnp.concatenate([p for p, _ in parts_a], axis=0)
            pb = jnp.concatenate([p for p, _ in parts_b], axis=0)
            ols = _dot(pa, va) + _dot(pb, vb)
            for i, c in enumerate(pairs):
                ol = ols[i * BLOCK:(i + 1) * BLOCK]
                denom = ol[:, LANES:] + jnp.where(low_head_b, parts_a[i][1], parts_b[i][1])
                gate = sza_scr[r0:r0 + BLOCK, c * LANES:(c + 1) * LANES]
                ub_scr[r0:r0 + BLOCK, c * LANES:(c + 1) * LANES] = (
                    gate * (ol[:, :LANES] * (1.0 / denom))).astype(_BF)
    for _ in dense:
        pass

    for r0, r1 in zip(EPILOGUE_SPLIT[:-1], EPILOGUE_SPLIT[1:]):
        rows = slice(r0, r1)
        for n in range(D_MODEL // MXU_N):
            cols = slice(n * MXU_N, (n + 1) * MXU_N)
            yb = _dot(ub_scr[rows, :], wpa_ref[:, cols])
            ua_scr[rows, cols] = (t_scr[rows, cols] + sgb_scr[rows, cols] * yb).astype(_BF)
        y = _dot(ua_scr[rows, :], wout_ref[...])
        ms2 = jnp.mean(y * y, axis=-1, keepdims=True)
        o_ref[0, rows, :] = x_ref[0, rows, :] + y * lax.rsqrt(ms2 + RMS_EPS) * gpost_ref[...]


def _rope_tables(t):
    inv_freq = ROPE_THETA ** (-jnp.arange(0, HEAD_DIM, 2, dtype=_F32) / HEAD_DIM)
    inv_freq = jnp.tile(inv_freq, 2 * PAIR)
    half = jnp.ones((HEAD_DIM // 2,), _F32)
    sign = jnp.tile(jnp.concatenate([-half, half]), PAIR)
    ang = jnp.arange(t).astype(_F32)[:, None] * inv_freq[None, :]
    return jnp.stack([jnp.cos(ang), jnp.sin(ang) * sign])


def _resident(shape):
    return pl.BlockSpec(shape, lambda b, j: (0,) * len(shape), pipeline_mode=pl.Buffered(1))


def _hybrid_layer(x, g_pre, g_post, w_in, w_conv, sinks, w_proj_conv, w_proj_attn, w_out):
    bsz, t, d = x.shape
    assert d == D_MODEL and t % TQ == 0 and w_in.shape == (D_MODEL, D_IN)
    tables = _rope_tables(t)
    return pl.pallas_call(
        _layer_kernel,
        out_shape=jax.ShapeDtypeStruct(x.shape, x.dtype),
        grid=(bsz, t // TQ),
        in_specs=[
            pl.BlockSpec(memory_space=pltpu.SMEM),
            pl.BlockSpec((1, TQ, D_MODEL), lambda b, j: (b, j, 0)),
            _resident((1, D_MODEL)),
            _resident((1, D_MODEL)),
            _resident((CONV_WIDTH, D_CONV)),
            pl.BlockSpec((2, TQ, LANES), lambda b, j: (0, j, 0)),
            _resident((D_MODEL, D_IN)),
            _resident((D_CONV, D_MODEL)),
            _resident((D_ATTN, D_MODEL)),
            _resident((D_MODEL, D_MODEL)),
        ],
        out_specs=pl.BlockSpec((1, TQ, D_MODEL), lambda b, j: (b, j, 0)),
        scratch_shapes=[
            pltpu.VMEM((TQ, D_MODEL), _BF),
            pltpu.VMEM((TQ, D_CONV), _BF),
            pltpu.VMEM((TQ, D_ATTN), _BF),
            pltpu.VMEM((TQ, D_ATTN), _BF),
            pltpu.VMEM((TQ, D_ATTN), _BF),
            pltpu.VMEM((2 * N_KV_HEADS, BLOCK + TQ, LANES), _BF),
            pltpu.VMEM((2 * N_KV_HEADS, BLOCK + TQ, 2 * LANES), _BF),
            pltpu.VMEM((SUBLANES, D_CONV), _F32),
            pltpu.VMEM((TQ, D_MODEL), _F32),
            pltpu.VMEM((TQ, D_MODEL), _BF),
        ],
        compiler_params=pltpu.CompilerParams(
            dimension_semantics=("arbitrary", "arbitrary"),
            vmem_limit_bytes=VMEM_LIMIT,
        ),
        name="hybrid_layer",
    )(sinks.astype(_F32), x, g_pre.reshape(1, D_MODEL), g_post.reshape(1, D_MODEL), w_conv,
      tables, w_in.astype(_BF), w_proj_conv.astype(_BF), w_proj_attn.astype(_BF),
      w_out.astype(_BF))


def kernel(x, g_pre, g_post, w_in, w_conv, sinks, w_proj_conv, w_proj_attn, w_out):
    for layer in range(g_pre.shape[0]):
        x = _hybrid_layer(x, g_pre[layer], g_post[layer], w_in[layer], w_conv[layer],
                          sinks[layer], w_proj_conv[layer], w_proj_attn[layer], w_out[layer])
    return x
```

```python
import jax
import jax.numpy as jnp
from jax import lax
from jax.experimental import pallas as pl
from jax.experimental.pallas import tpu as pltpu

D_MODEL = 1024
D_CONV = D_MODEL
CONV_WIDTH = 3
HEAD_DIM = 64
N_HEADS = D_MODEL // HEAD_DIM
N_KV_HEADS = 2
GROUP = N_HEADS // N_KV_HEADS
D_ATTN = N_HEADS * HEAD_DIM
D_KV = N_KV_HEADS * HEAD_DIM
WINDOW = 128
BLOCK = WINDOW
ROPE_THETA = 10000.0
RMS_EPS = 1e-6

OFF_XC = 0
OFF_BG = OFF_XC + D_CONV
OFF_CG = OFF_BG + D_CONV
OFF_ZC = OFF_CG + D_CONV
OFF_Q = OFF_ZC + D_CONV
OFF_K = OFF_Q + D_ATTN
OFF_V = OFF_K + D_KV
OFF_ZA = OFF_V + D_KV
OFF_GA = OFF_ZA + D_ATTN
OFF_GB = OFF_GA + D_MODEL
D_IN = OFF_GB + D_MODEL

LANES = 128
SUBLANES = 8
MXU_N = 256
PAIR = LANES // HEAD_DIM
N_PAIRS = N_HEADS // PAIR
PAIRS_PER_KV = GROUP // PAIR
PAIR_STACK = 2
N_DENSE_UNITS = (2 * (D_ATTN // MXU_N - 1) + 4 * (D_CONV // MXU_N) + 2 * (D_MODEL // MXU_N)
                 + D_MODEL // MXU_N)
TQ = 1024
DOT_ROWS = 512
EPILOGUE_ROWS = 512
EPILOGUE_SPLIT = tuple(range(0, TQ + 1, EPILOGUE_ROWS))
NEG = -1e30
VMEM_LIMIT = 62 * 1024 * 1024

assert D_KV == LANES and PAIR == 2 and PAIRS_PER_KV % PAIR_STACK == 0 and D_MODEL % MXU_N == 0
assert TQ % BLOCK == 0 and TQ % DOT_ROWS == 0 and TQ % EPILOGUE_ROWS == 0

_BF = jnp.bfloat16
_F32 = jnp.float32


def _dot(a, b):
    m = a.shape[0]
    if m <= DOT_ROWS:
        return jnp.dot(a, b, preferred_element_type=_F32)
    return jnp.concatenate(
        [jnp.dot(a[r:r + DOT_ROWS], b, preferred_element_type=_F32) for r in range(0, m, DOT_ROWS)],
        axis=0)


def _dot_nt(a, b):
    return lax.dot_general(a, b, (((1,), (1,)), ((), ())), preferred_element_type=_F32)


def _sigmoid(z):
    return 1.0 / (1.0 + jnp.exp(-z))


def _layer_kernel(sink_ref, x_ref, gpre_ref, gpost_ref, wconv_ref, tab_ref,
                  win_ref, wpc_ref, wpa_ref, wout_ref, o_ref,
                  h_scr, ua_scr, ub_scr, q_scr, sza_scr, k_scr, v_scr, ucarry, t_scr, sgb_scr):
    j = pl.program_id(1)

    @pl.when((pl.program_id(0) == 0) & (j == 0))
    def _():
        low = lax.broadcasted_iota(jnp.int32, (BLOCK + TQ, LANES), 1) < HEAD_DIM
        ones_lo = jnp.where(low, 1.0, 0.0).astype(_BF)
        ones_hi = jnp.where(low, 0.0, 1.0).astype(_BF)
        for i in range(2 * N_KV_HEADS):
            v_scr[i, :, LANES:] = ones_hi if i % 2 else ones_lo

    @pl.when(j == 0)
    def _():
        ucarry[...] = jnp.zeros_like(ucarry)
        k_scr[:, 0:BLOCK, :] = jnp.zeros((2 * N_KV_HEADS, BLOCK, LANES), _BF)
        v_scr[:, 0:BLOCK, 0:LANES] = jnp.zeros((2 * N_KV_HEADS, BLOCK, LANES), _BF)

    @pl.when(j != 0)
    def _():
        k_scr[:, 0:BLOCK, :] = k_scr[:, TQ:TQ + BLOCK, :]
        v_scr[:, 0:BLOCK, 0:LANES] = v_scr[:, TQ:TQ + BLOCK, 0:LANES]

    x = x_ref[0]
    ms = jnp.mean(x * x, axis=-1, keepdims=True)
    h_scr[...] = (x * lax.rsqrt(ms + RMS_EPS) * gpre_ref[...]).astype(_BF)

    def proj(off, n):
        lo = off + n * MXU_N
        return _dot(h_scr[...], win_ref[:, lo:lo + MXU_N])

    lane = lax.broadcasted_iota(jnp.int32, (TQ, LANES), 1)
    first_half = (lane & (HEAD_DIM // 2)) == 0
    low_head = lane < HEAD_DIM

    def rope(z, cos, sin):
        rot = jnp.where(first_half, pltpu.roll(z, LANES - HEAD_DIM // 2, 1),
                        pltpu.roll(z, HEAD_DIM // 2, 1))
        return z * cos + rot * sin

    def q_unit(n):
        qf = proj(OFF_Q, n) * (HEAD_DIM ** -0.5)
        for c in range(MXU_N // LANES):
            qc = rope(qf[:, c * LANES:(c + 1) * LANES], tab_ref[0], tab_ref[1])
            lo = n * MXU_N + c * LANES
            q_scr[:, lo:lo + LANES] = qc.astype(_BF)

    def za_unit(n):
        za = proj(OFF_ZA, n)
        sza_scr[:, n * MXU_N:(n + 1) * MXU_N] = (za * _sigmoid(za)).astype(_BF)

    row = lax.broadcasted_iota(jnp.int32, (TQ, MXU_N), 0)

    def dense_units():
        for n in range(1, D_ATTN // MXU_N):
            q_unit(n)
            yield
            za_unit(n)
            yield
        for n in range(D_CONV // MXU_N):
            cols = slice(n * MXU_N, (n + 1) * MXU_N)
            xc = proj(OFF_XC, n)
            yield
            u = proj(OFF_CG, n) * xc
            prev = ucarry[:, cols]
            p1 = prev[SUBLANES - 1:SUBLANES, :]
            p2 = prev[SUBLANES - 2:SUBLANES - 1, :]
            u1 = jnp.where(row == 0, p1, pltpu.roll(u, 1, 0))
            u2 = jnp.where(row == 0, p2, jnp.where(row == 1, p1, pltpu.roll(u, 2, 0)))
            ucarry[:, cols] = u[TQ - SUBLANES:, :]
            w = wconv_ref[:, cols]
            y = w[0:1, :] * u2 + w[1:2, :] * u1 + w[2:3, :] * u
            yield
            y = proj(OFF_BG, n) * y
            yield
            zc = proj(OFF_ZC, n)
            ua_scr[:, cols] = ((zc * _sigmoid(zc)) * y).astype(_BF)
            yield
        for n in range(D_MODEL // MXU_N):
            cols = slice(n * MXU_N, (n + 1) * MXU_N)
            ya = _dot(ua_scr[...], wpc_ref[:, cols])
            yield
            t_scr[:, cols] = _sigmoid(proj(OFF_GA, n)) * ya
            yield
        for n in range(D_MODEL // MXU_N):
            sgb_scr[:, n * MXU_N:(n + 1) * MXU_N] = _sigmoid(proj(OFF_GB, n)).astype(_BF)
            yield

    kv = proj(OFF_K, 0)
    k = rope(kv[:, :D_KV], tab_ref[0], tab_ref[1])
    v = kv[:, D_KV:]
    kr = pltpu.roll(k, HEAD_DIM, 1)
    vr = pltpu.roll(v, HEAD_DIM, 1)
    zero = jnp.zeros_like(k)
    k_variants = (jnp.where(low_head, k, zero), jnp.where(low_head, zero, kr),
                  jnp.where(low_head, kr, zero), jnp.where(low_head, zero, k))
    v_variants = (jnp.where(low_head, v, zero), jnp.where(low_head, zero, vr),
                  jnp.where(low_head, vr, zero), jnp.where(low_head, zero, v))
    for i in range(2 * N_KV_HEADS):
        k_scr[i, BLOCK:BLOCK + TQ, :] = k_variants[i].astype(_BF)
        v_scr[i, BLOCK:BLOCK + TQ, 0:LANES] = v_variants[i].astype(_BF)
    q_unit(0)
    za_unit(0)

    qi = lax.broadcasted_iota(jnp.int32, (BLOCK, 2 * BLOCK), 0)
    kj = lax.broadcasted_iota(jnp.int32, (BLOCK, 2 * BLOCK), 1)
    band = (kj > qi) & (kj <= qi + BLOCK)
    low_head_b = lax.broadcasted_iota(jnp.int32, (BLOCK, LANES), 1) < HEAD_DIM

    def softmax_parts(sc, mask, sink):
        sc = jnp.where(mask, sc, NEG)
        m = jnp.maximum(jnp.max(sc, axis=-1, keepdims=True), sink)
        return jnp.exp(sc - m).astype(_BF), jnp.exp(sink - m)

    dense = dense_units()
    n_iters = (N_PAIRS // PAIR_STACK) * (TQ // BLOCK)
    n_left = N_DENSE_UNITS
    it = 0
    for c0 in range(0, N_PAIRS, PAIR_STACK):
        kvh = c0 // PAIRS_PER_KV
        pairs = range(c0, c0 + PAIR_STACK)
        for b in range(TQ // BLOCK):
            r0 = b * BLOCK
            first_key = jnp.where(j == 0, BLOCK, 0) if b == 0 else 0
            mask = band & (kj >= first_key)
            qs = jnp.concatenate(
                [q_scr[r0:r0 + BLOCK, c * LANES:(c + 1) * LANES] for c in pairs], axis=0)
            ka = k_scr[2 * kvh, r0:r0 + 2 * BLOCK, :]
            kb = k_scr[2 * kvh + 1, r0:r0 + 2 * BLOCK, :]
            va = v_scr[2 * kvh, r0:r0 + 2 * BLOCK, :]
            vb = v_scr[2 * kvh + 1, r0:r0 + 2 * BLOCK, :]
            sa = _dot_nt(qs, ka)
            sb = _dot_nt(qs, kb)
            for _ in range(n_left * (it + 1) // n_iters - n_left * it // n_iters):
                next(dense)
            it += 1
            parts_a = [softmax_parts(sa[i * BLOCK:(i + 1) * BLOCK], mask, sink_ref[PAIR * c])
                       for i, c in enumerate(pairs)]
            parts_b = [softmax_parts(sb[i * BLOCK:(i + 1) * BLOCK], mask, sink_ref[PAIR * c + 1])
                       for i, c in enumerate(pairs)]
            pa = jnp.concatenate([p for p, _ in parts_a], axis=0)
            pb = jnp.concatenate([p for p, _ in parts_b], axis=0)
            ols = _dot(pa, va) + _dot(pb, vb)
            for i, c in enumerate(pairs):
                ol = ols[i * BLOCK:(i + 1) * BLOCK]
                denom = ol[:, LANES:] + jnp.where(low_head_b, parts_a[i][1], parts_b[i][1])
                gate = sza_scr[r0:r0 + BLOCK, c * LANES:(c + 1) * LANES]
                ub_scr[r0:r0 + BLOCK, c * LANES:(c + 1) * LANES] = (
                    gate * (ol[:, :LANES] * (1.0 / denom))).astype(_BF)
    for _ in dense:
        pass

    for r0, r1 in zip(EPILOGUE_SPLIT[:-1], EPILOGUE_SPLIT[1:]):
        rows = slice(r0, r1)
        for n in range(D_MODEL // MXU_N):
            cols = slice(n * MXU_N, (n + 1) * MXU_N)
            yb = _dot(ub_scr[rows, :], wpa_ref[:, cols])
            ua_scr[rows, cols] = (t_scr[rows, cols] + sgb_scr[rows, cols] * yb).astype(_BF)
        y = _dot(ua_scr[rows, :], wout_ref[...])
        ms2 = jnp.mean(y * y, axis=-1, keepdims=True)
        o_ref[0, rows, :] = x_ref[0, rows, :] + y * lax.rsqrt(ms2 + RMS_EPS) * gpost_ref[...]


def _rope_tables(t):
    inv_freq = ROPE_THETA ** (-jnp.arange(0, HEAD_DIM, 2, dtype=_F32) / HEAD_DIM)
    inv_freq = jnp.tile(inv_freq, 2 * PAIR)
    half = jnp.ones((HEAD_DIM // 2,), _F32)
    sign = jnp.tile(jnp.concatenate([-half, half]), PAIR)
    ang = jnp.arange(t).astype(_F32)[:, None] * inv_freq[None, :]
    return jnp.stack([jnp.cos(ang), jnp.sin(ang) * sign])


def _resident(shape):
    return pl.BlockSpec(shape, lambda b, j: (0,) * len(shape), pipeline_mode=pl.Buffered(1))


def _hybrid_layer(x, g_pre, g_post, w_in, w_conv, sinks, w_proj_conv, w_proj_attn, w_out):
    bsz, t, d = x.shape
    assert d == D_MODEL and t % TQ == 0 and w_in.shape == (D_MODEL, D_IN)
    tables = _rope_tables(t)
    return pl.pallas_call(
        _layer_kernel,
        out_shape=jax.ShapeDtypeStruct(x.shape, x.dtype),
        grid=(bsz, t // TQ),
        in_specs=[
            pl.BlockSpec(memory_space=pltpu.SMEM),
            pl.BlockSpec((1, TQ, D_MODEL), lambda b, j: (b, j, 0)),
            _resident((1, D_MODEL)),
            _resident((1, D_MODEL)),
            _resident((CONV_WIDTH, D_CONV)),
            pl.BlockSpec((2, TQ, LANES), lambda b, j: (0, j, 0)),
            _resident((D_MODEL, D_IN)),
            _resident((D_CONV, D_MODEL)),
            _resident((D_ATTN, D_MODEL)),
            _resident((D_MODEL, D_MODEL)),
        ],
        out_specs=pl.BlockSpec((1, TQ, D_MODEL), lambda b, j: (b, j, 0)),
        scratch_shapes=[
            pltpu.VMEM((TQ, D_MODEL), _BF),
            pltpu.VMEM((TQ, D_CONV), _BF),
            pltpu.VMEM((TQ, D_ATTN), _BF),
            pltpu.VMEM((TQ, D_ATTN), _BF),
            pltpu.VMEM((TQ, D_ATTN), _BF),
            pltpu.VMEM((2 * N_KV_HEADS, BLOCK + TQ, LANES), _BF),
            pltpu.VMEM((2 * N_KV_HEADS, BLOCK + TQ, 2 * LANES), _BF),
            pltpu.VMEM((SUBLANES, D_CONV), _F32),
            pltpu.VMEM((TQ, D_MODEL), _F32),
            pltpu.VMEM((TQ, D_MODEL), _BF),
        ],
        compiler_params=pltpu.CompilerParams(
            dimension_semantics=("arbitrary", "arbitrary"),
            vmem_limit_bytes=VMEM_LIMIT,
        ),
        name="hybrid_layer",
    )(sinks.astype(_F32), x, g_pre.reshape(1, D_MODEL), g_post.reshape(1, D_MODEL), w_conv,
      tables, w_in.astype(_BF), w_proj_conv.astype(_BF), w_proj_attn.astype(_BF),
      w_out.astype(_BF))


def kernel(x, g_pre, g_post, w_in, w_conv, sinks, w_proj_conv, w_proj_attn, w_out):
    for layer in range(g_pre.shape[0]):
        x = _hybrid_layer(x, g_pre[layer], g_post[layer], w_in[layer], w_conv[layer],
                          sinks[layer], w_proj_conv[layer], w_proj_attn[layer], w_out[layer])
    return x
```

```python
import jax
import jax.numpy as jnp
from jax import lax
from jax.experimental import pallas as pl
from jax.experimental.pallas import tpu as pltpu

D_MODEL = 1024
D_CONV = D_MODEL
CONV_WIDTH = 3
HEAD_DIM = 64
N_HEADS = D_MODEL // HEAD_DIM
N_KV_HEADS = 2
GROUP = N_HEADS // N_KV_HEADS
D_ATTN = N_HEADS * HEAD_DIM
D_KV = N_KV_HEADS * HEAD_DIM
WINDOW = 128
BLOCK = WINDOW
ROPE_THETA = 10000.0
RMS_EPS = 1e-6

OFF_XC = 0
OFF_BG = OFF_XC + D_CONV
OFF_CG = OFF_BG + D_CONV
OFF_ZC = OFF_CG + D_CONV
OFF_Q = OFF_ZC + D_CONV
OFF_K = OFF_Q + D_ATTN
OFF_V = OFF_K + D_KV
OFF_ZA = OFF_V + D_KV
OFF_GA = OFF_ZA + D_ATTN
OFF_GB = OFF_GA + D_MODEL
D_IN = OFF_GB + D_MODEL

LANES = 128
SUBLANES = 8
MXU_N = 256
PAIR = LANES // HEAD_DIM
N_PAIRS = N_HEADS // PAIR
PAIRS_PER_KV = GROUP // PAIR
PAIR_STACK = 2
N_DENSE_UNITS = (2 * (D_ATTN // MXU_N - 1) + 4 * (D_CONV // MXU_N) + 2 * (D_MODEL // MXU_N)
                 + D_MODEL // MXU_N)
TQ = 1024
DOT_ROWS = 512
EPILOGUE_ROWS = 256
EPILOGUE_SPLIT = tuple(range(0, TQ + 1, EPILOGUE_ROWS))
NEG = -1e30
VMEM_LIMIT = 62 * 1024 * 1024

assert D_KV == LANES and PAIR == 2 and PAIRS_PER_KV % PAIR_STACK == 0 and D_MODEL % MXU_N == 0
assert TQ % BLOCK == 0 and TQ % DOT_ROWS == 0 and TQ % EPILOGUE_ROWS == 0

_BF = jnp.bfloat16
_F32 = jnp.float32


def _dot(a, b):
    m = a.shape[0]
    if m <= DOT_ROWS:
        return jnp.dot(a, b, preferred_element_type=_F32)
    return jnp.concatenate(
        [jnp.dot(a[r:r + DOT_ROWS], b, preferred_element_type=_F32) for r in range(0, m, DOT_ROWS)],
        axis=0)


def _dot_nt(a, b):
    return lax.dot_general(a, b, (((1,), (1,)), ((), ())), preferred_element_type=_F32)


def _sigmoid(z):
    return 1.0 / (1.0 + jnp.exp(-z))


def _layer_kernel(sink_ref, x_ref, gpre_ref, gpost_ref, wconv_ref, tab_ref,
                  win_ref, wpc_ref, wpa_ref, wout_ref, o_ref,
                  h_scr, ua_scr, ub_scr, q_scr, sza_scr, k_scr, v_scr, ucarry, t_scr, sgb_scr):
    j = pl.program_id(1)

    @pl.when((pl.program_id(0) == 0) & (j == 0))
    def _():
        low = lax.broadcasted_iota(jnp.int32, (BLOCK + TQ, LANES), 1) < HEAD_DIM
        ones_lo = jnp.where(low, 1.0, 0.0).astype(_BF)
        ones_hi = jnp.where(low, 0.0, 1.0).astype(_BF)
        for i in range(2 * N_KV_HEADS):
            v_scr[i, :, LANES:] = ones_hi if i % 2 else ones_lo

    @pl.when(j == 0)
    def _():
        ucarry[...] = jnp.zeros_like(ucarry)
        k_scr[:, 0:BLOCK, :] = jnp.zeros((2 * N_KV_HEADS, BLOCK, LANES), _BF)
        v_scr[:, 0:BLOCK, 0:LANES] = jnp.zeros((2 * N_KV_HEADS, BLOCK, LANES), _BF)

    @pl.when(j != 0)
    def _():
        k_scr[:, 0:BLOCK, :] = k_scr[:, TQ:TQ + BLOCK, :]
        v_scr[:, 0:BLOCK, 0:LANES] = v_scr[:, TQ:TQ + BLOCK, 0:LANES]

    x = x_ref[0]
    ms = jnp.mean(x * x, axis=-1, keepdims=True)
    h_scr[...] = (x * lax.rsqrt(ms + RMS_EPS) * gpre_ref[...]).astype(_BF)

    def proj(off, n):
        lo = off + n * MXU_N
        return _dot(h_scr[...], win_ref[:, lo:lo + MXU_N])

    lane = lax.broadcasted_iota(jnp.int32, (TQ, LANES), 1)
    first_half = (lane & (HEAD_DIM // 2)) == 0
    low_head = lane < HEAD_DIM

    def rope(z, cos, sin):
        rot = jnp.where(first_half, pltpu.roll(z, LANES - HEAD_DIM // 2, 1),
                        pltpu.roll(z, HEAD_DIM // 2, 1))
        return z * cos + rot * sin

    def q_unit(n):
        qf = proj(OFF_Q, n) * (HEAD_DIM ** -0.5)
        for c in range(MXU_N // LANES):
            qc = rope(qf[:, c * LANES:(c + 1) * LANES], tab_ref[0], tab_ref[1])
            lo = n * MXU_N + c * LANES
            q_scr[:, lo:lo + LANES] = qc.astype(_BF)

    def za_unit(n):
        za = proj(OFF_ZA, n)
        sza_scr[:, n * MXU_N:(n + 1) * MXU_N] = (za * _sigmoid(za)).astype(_BF)

    row = lax.broadcasted_iota(jnp.int32, (TQ, MXU_N), 0)

    def dense_units():
        for n in range(1, D_ATTN // MXU_N):
            q_unit(n)
            yield
            za_unit(n)
            yield
        for n in range(D_CONV // MXU_N):
            cols = slice(n * MXU_N, (n + 1) * MXU_N)
            xc = proj(OFF_XC, n)
            yield
            u = proj(OFF_CG, n) * xc
            prev = ucarry[:, cols]
            p1 = prev[SUBLANES - 1:SUBLANES, :]
            p2 = prev[SUBLANES - 2:SUBLANES - 1, :]
            u1 = jnp.where(row == 0, p1, pltpu.roll(u, 1, 0))
            u2 = jnp.where(row == 0, p2, jnp.where(row == 1, p1, pltpu.roll(u, 2, 0)))
            ucarry[:, cols] = u[TQ - SUBLANES:, :]
            w = wconv_ref[:, cols]
            y = w[0:1, :] * u2 + w[1:2, :] * u1 + w[2:3, :] * u
            yield
            y = proj(OFF_BG, n) * y
            yield
            zc = proj(OFF_ZC, n)
            ua_scr[:, cols] = ((zc * _sigmoid(zc)) * y).astype(_BF)
            yield
        for n in range(D_MODEL // MXU_N):
            cols = slice(n * MXU_N, (n + 1) * MXU_N)
            ya = _dot(ua_scr[...], wpc_ref[:, cols])
            yield
            t_scr[:, cols] = _sigmoid(proj(OFF_GA, n)) * ya
            yield
        for n in range(D_MODEL // MXU_N):
            sgb_scr[:, n * MXU_N:(n + 1) * MXU_N] = _sigmoid(proj(OFF_GB, n)).astype(_BF)
            yield

    kv = proj(OFF_K, 0)
    k = rope(kv[:, :D_KV], tab_ref[0], tab_ref[1])
    v = kv[:, D_KV:]
    kr = pltpu.roll(k, HEAD_DIM, 1)
    vr = pltpu.roll(v, HEAD_DIM, 1)
    zero = jnp.zeros_like(k)
    k_variants = (jnp.where(low_head, k, zero), jnp.where(low_head, zero, kr),
                  jnp.where(low_head, kr, zero), jnp.where(low_head, zero, k))
    v_variants = (jnp.where(low_head, v, zero), jnp.where(low_head, zero, vr),
                  jnp.where(low_head, vr, zero), jnp.where(low_head, zero, v))
    for i in range(2 * N_KV_HEADS):
        k_scr[i, BLOCK:BLOCK + TQ, :] = k_variants[i].astype(_BF)
        v_scr[i, BLOCK:BLOCK + TQ, 0:LANES] = v_variants[i].astype(_BF)
    q_unit(0)
    za_unit(0)

    qi = lax.broadcasted_iota(jnp.int32, (BLOCK, 2 * BLOCK), 0)
    kj = lax.broadcasted_iota(jnp.int32, (BLOCK, 2 * BLOCK), 1)
    band = (kj > qi) & (kj <= qi + BLOCK)
    low_head_b = lax.broadcasted_iota(jnp.int32, (BLOCK, LANES), 1) < HEAD_DIM

    def softmax_parts(sc, mask, sink):
        sc = jnp.where(mask, sc, NEG)
        m = jnp.maximum(jnp.max(sc, axis=-1, keepdims=True), sink)
        return jnp.exp(sc - m).astype(_BF), jnp.exp(sink - m)

    dense = dense_units()
    n_iters = (N_PAIRS // PAIR_STACK) * (TQ // BLOCK)
    n_left = N_DENSE_UNITS
    it = 0
    for c0 in range(0, N_PAIRS, PAIR_STACK):
        kvh = c0 // PAIRS_PER_KV
        pairs = range(c0, c0 + PAIR_STACK)
        for b in range(TQ // BLOCK):
            r0 = b * BLOCK
            first_key = jnp.where(j == 0, BLOCK, 0) if b == 0 else 0
            mask = band & (kj >= first_key)
            qs = jnp.concatenate(
                [q_scr[r0:r0 + BLOCK, c * LANES:(c + 1) * LANES] for c in pairs], axis=0)
            ka = k_scr[2 * kvh, r0:r0 + 2 * BLOCK, :]
            kb = k_scr[2 * kvh + 1, r0:r0 + 2 * BLOCK, :]
            va = v_scr[2 * kvh, r0:r0 + 2 * BLOCK, :]
            vb = v_scr[2 * kvh + 1, r0:r0 + 2 * BLOCK, :]
            sa = _dot_nt(qs, ka)
            sb = _dot_nt(qs, kb)
            for _ in range(n_left * (it + 1) // n_iters - n_left * it // n_iters):
                next(dense)
            it += 1
            parts_a = [softmax_parts(sa[i * BLOCK:(i + 1) * BLOCK], mask, sink_ref[PAIR * c])
                       for i, c in enumerate(pairs)]
            parts_b = [softmax_parts(sb[i * BLOCK:(i + 1) * BLOCK], mask, sink_ref[PAIR * c + 1])
                       for i, c in enumerate(pairs)]
            pa = jnp.concatenate([p for p, _ in parts_a], axis=0)
            pb = jnp.concatenate([p for p, _ in parts_b], axis=0)
            ols = _dot(pa, va) + _dot(pb, vb)
            for i, c in enumerate(pairs):
                ol = ols[i * BLOCK:(i + 1) * BLOCK]
                denom = ol[:, LANES:] + jnp.where(low_head_b, parts_a[i][1], parts_b[i][1])
                gate = sza_scr[r0:r0 + BLOCK, c * LANES:(c + 1) * LANES]
                ub_scr[r0:r0 + BLOCK, c * LANES:(c + 1) * LANES] = (
                    gate * (ol[:, :LANES] * (1.0 / denom))).astype(_BF)
    for _ in dense:
        pass

    for r0, r1 in zip(EPILOGUE_SPLIT[:-1], EPILOGUE_SPLIT[1:]):
        rows = slice(r0, r1)
        for n in range(D_MODEL // MXU_N):
            cols = slice(n * MXU_N, (n + 1) * MXU_N)
            yb = _dot(ub_scr[rows, :], wpa_ref[:, cols])
            ua_scr[rows, cols] = (t_scr[rows, cols] + sgb_scr[rows, cols] * yb).astype(_BF)
        ssq = jnp.zeros((r1 - r0, 1), _F32)
        for n in range(D_MODEL // MXU_N):
            cols = slice(n * MXU_N, (n + 1) * MXU_N)
            yc = _dot(ua_scr[rows, :], wout_ref[:, cols])
            t_scr[rows, cols] = yc
            ssq = ssq + jnp.sum(yc * yc, axis=-1, keepdims=True)
        rnorm = lax.rsqrt(ssq * (1.0 / D_MODEL) + RMS_EPS)
        o_ref[0, rows, :] = x_ref[0, rows, :] + t_scr[rows, :] * rnorm * gpost_ref[...]


def _rope_tables(t):
    inv_freq = ROPE_THETA ** (-jnp.arange(0, HEAD_DIM, 2, dtype=_F32) / HEAD_DIM)
    inv_freq = jnp.tile(inv_freq, 2 * PAIR)
    half = jnp.ones((HEAD_DIM // 2,), _F32)
    sign = jnp.tile(jnp.concatenate([-half, half]), PAIR)
    ang = jnp.arange(t).astype(_F32)[:, None] * inv_freq[None, :]
    return jnp.stack([jnp.cos(ang), jnp.sin(ang) * sign])


def _resident(shape):
    return pl.BlockSpec(shape, lambda b, j: (0,) * len(shape), pipeline_mode=pl.Buffered(1))


def _hybrid_layer(x, g_pre, g_post, w_in, w_conv, sinks, w_proj_conv, w_proj_attn, w_out):
    bsz, t, d = x.shape
    assert d == D_MODEL and t % TQ == 0 and w_in.shape == (D_MODEL, D_IN)
    tables = _rope_tables(t)
    return pl.pallas_call(
        _layer_kernel,
        out_shape=jax.ShapeDtypeStruct(x.shape, x.dtype),
        grid=(bsz, t // TQ),
        in_specs=[
            pl.BlockSpec(memory_space=pltpu.SMEM),
            pl.BlockSpec((1, TQ, D_MODEL), lambda b, j: (b, j, 0)),
            _resident((1, D_MODEL)),
            _resident((1, D_MODEL)),
            _resident((CONV_WIDTH, D_CONV)),
            pl.BlockSpec((2, TQ, LANES), lambda b, j: (0, j, 0)),
            _resident((D_MODEL, D_IN)),
            _resident((D_CONV, D_MODEL)),
            _resident((D_ATTN, D_MODEL)),
            _resident((D_MODEL, D_MODEL)),
        ],
        out_specs=pl.BlockSpec((1, TQ, D_MODEL), lambda b, j: (b, j, 0)),
        scratch_shapes=[
            pltpu.VMEM((TQ, D_MODEL), _BF),
            pltpu.VMEM((TQ, D_CONV), _BF),
            pltpu.VMEM((TQ, D_ATTN), _BF),
            pltpu.VMEM((TQ, D_ATTN), _BF),
            pltpu.VMEM((TQ, D_ATTN), _BF),
            pltpu.VMEM((2 * N_KV_HEADS, BLOCK + TQ, LANES), _BF),
            pltpu.VMEM((2 * N_KV_HEADS, BLOCK + TQ, 2 * LANES), _BF),
            pltpu.VMEM((SUBLANES, D_CONV), _F32),
            pltpu.VMEM((TQ, D_MODEL), _F32),
            pltpu.VMEM((TQ, D_MODEL), _BF),
        ],
        compiler_params=pltpu.CompilerParams(
            dimension_semantics=("arbitrary", "arbitrary"),
            vmem_limit_bytes=VMEM_LIMIT,
        ),
        name="hybrid_layer",
    )(sinks.astype(_F32), x, g_pre.reshape(1, D_MODEL), g_post.reshape(1, D_MODEL), w_conv,
      tables, w_in.astype(_BF), w_proj_conv.astype(_BF), w_proj_attn.astype(_BF),
      w_out.astype(_BF))


def kernel(x, g_pre, g_post, w_in, w_conv, sinks, w_proj_conv, w_proj_attn, w_out):
    for layer in range(g_pre.shape[0]):
        x = _hybrid_layer(x, g_pre[layer], g_post[layer], w_in[layer], w_conv[layer],
                          sinks[layer], w_proj_conv[layer], w_proj_attn[layer], w_out[layer])
    return x
```
